```python
import math
import jax, jax.numpy as jnp
from jax import lax
import numpy as np

D_MODEL = 1024
BATCH = 8
SEQ = 2048
DEPTH = 1
DEC_BATCH = 128
DEC_SEQ = 4
PAST_LEN = 16384
PAGE_SIZE = 128

D_MIX = D_MODEL
D_CONV = D_MIX // 2
N_CONV_GROUPS = 8
CONV_W = 3
D_HGRN = D_MIX - D_CONV
H_HGRN = 4
DK = 128
DV = D_HGRN // H_HGRN
D_FF = 2816
CHUNK = 64
ALPHA = (2.0 * DEPTH) ** 0.25
BETA = (8.0 * DEPTH) ** -0.25
EPS = 1e-5

kernel_name = 'hybrid_conv_hgrn2_macaron_step'


def layer_norm(x, g, b):
    xf = x.astype(jnp.float32)
    mu = jnp.mean(xf, axis=-1, keepdims=True)
    var = jnp.mean(jnp.square(xf - mu), axis=-1, keepdims=True)
    y = (xf - mu) * lax.rsqrt(var + EPS)
    return (y * g + b).astype(x.dtype)


def group_rms_norm(x, n_groups, g):
    B_, L, W = x.shape
    xg = x.reshape(B_, L, n_groups, W // n_groups)
    xg = xg * lax.rsqrt(jnp.mean(jnp.square(xg), axis=-1, keepdims=True) + EPS)
    return xg.reshape(B_, L, W) * g


def swiglu_ffn(x, w_up, w_down):
    a, b = jnp.split(x @ w_up, 2, axis=-1)
    return (jax.nn.silu(a) * b) @ w_down


def hgrn2_chunked(q, k, logf, v, S0):
    B_, L = q.shape[0], q.shape[1]
    C = math.gcd(L, CHUNK)
    N = L // C

    def to_chunks(t):
        return t.reshape(B_, N, C, H_HGRN, t.shape[-1]).transpose(1, 0, 3, 2, 4)

    mask = jnp.tril(jnp.ones((C, C), dtype=bool))[:, :, None]

    def step(S, inp):
        qc, kc, gc, vc = inp
        b = jnp.cumsum(gc, axis=2)
        decay = jnp.exp(jnp.where(mask, b[:, :, :, None, :] - b[:, :, None, :, :], -jnp.inf))
        scores = jnp.einsum('bhtk,bhtsk,bhsk->bhts', qc, decay, kc)
        o = (jnp.einsum('bhts,bhsv->bhtv', scores, vc)
             + jnp.einsum('bhtk,bhkv->bhtv', qc * jnp.exp(b), S))
        b_end = b[:, :, -1:, :]
        S_new = (jnp.exp(b_end[:, :, 0, :])[..., None] * S
                 + jnp.einsum('bhsk,bhsv->bhkv', kc * jnp.exp(b_end - b), vc))
        return S_new, o

    S_fin, o = lax.scan(step, S0, (to_chunks(q), to_chunks(k), to_chunks(logf), to_chunks(v)))
    o = o.transpose(1, 0, 3, 2, 4).reshape(B_, L, H_HGRN, DV)
    return o, S_fin


def mixing_sublayer(x, conv_state, hgrn_state, lb, w_in, conv_w, conv_norm_g, hgrn_norm_g, w_o):
    B_, L, _ = x.shape
    HK, HV = H_HGRN * DK, H_HGRN * DV
    splits = [D_CONV, 2 * D_CONV, 3 * D_CONV, 3 * D_CONV + HK, 3 * D_CONV + 2 * HK, 3 * D_CONV + 2 * HK + HV]
    h, c_gate, b_gate, q, f_logit, i_val, g_out = jnp.split(x @ w_in, splits, axis=-1)

    u = c_gate * h
    full = jnp.concatenate([conv_state.astype(u.dtype), u], axis=1)
    conv_out = full[:, 0:L] * conv_w[0]
    for j in range(1, CONV_W):
        conv_out = conv_out + full[:, j:j + L] * conv_w[j]
    new_conv = full[:, L:]
    y_conv = group_rms_norm((b_gate * conv_out).astype(jnp.float32), N_CONV_GROUPS, conv_norm_g)

    sig = jax.nn.sigmoid(f_logit.astype(jnp.float32))
    f = lb + (1.0 - lb) * sig
    logf = jnp.log(f).reshape(B_, L, H_HGRN, DK)
    k = (1.0 - f).reshape(B_, L, H_HGRN, DK)
    qf = jax.nn.silu(q.astype(jnp.float32)).reshape(B_, L, H_HGRN, DK)
    vf = i_val.astype(jnp.float32).reshape(B_, L, H_HGRN, DV)
    o, new_hgrn = hgrn2_chunked(qf, k, logf, vf, hgrn_state.astype(jnp.float32))
    y_hgrn = group_rms_norm(o.reshape(B_, L, HV), H_HGRN, hgrn_norm_g) * jax.nn.silu(g_out.astype(jnp.float32))

    y = jnp.concatenate([y_conv, y_hgrn], axis=-1).astype(x.dtype) @ w_o
    return y, new_conv, new_hgrn


def decoder_layer(x, conv_state, hgrn_state, lb, ln1_g, ln1_b, ffn1_w_up, ffn1_w_down, ln2_g, ln2_b,
                  w_in, conv_w, conv_norm_g, hgrn_norm_g, w_o, ln3_g, ln3_b, ffn2_w_up, ffn2_w_down):
    x = layer_norm(ALPHA * x + 0.5 * swiglu_ffn(x, ffn1_w_up, ffn1_w_down), ln1_g, ln1_b)
    m, new_conv, new_hgrn = mixing_sublayer(x, conv_state, hgrn_state, lb, w_in, conv_w,
                                            conv_norm_g, hgrn_norm_g, w_o)
    x = layer_norm(ALPHA * x + m, ln2_g, ln2_b)
    x = layer_norm(ALPHA * x + 0.5 * swiglu_ffn(x, ffn2_w_up, ffn2_w_down), ln3_g, ln3_b)
    return x, new_conv, new_hgrn


def setup_inputs(seed: int = 0) -> dict:
    key = jax.random.key(seed)
    ks = jax.random.split(key, 24)
    f32 = jnp.float32
    nrm = lambda k, shape, s: jax.random.normal(k, shape, f32) * s
    HK, HV = H_HGRN * DK, H_HGRN * DV
    d_in = 3 * D_CONV + 2 * HK + 2 * HV
    col_scale = jnp.concatenate([jnp.full((D_CONV,), BETA, f32), jnp.ones((2 * D_CONV + 2 * HK,), f32),
                                 jnp.full((HV,), BETA, f32), jnp.ones((HV,), f32)])
    return {
        'x_prompt': nrm(ks[0], (BATCH, SEQ, D_MODEL), 1.0),
        'x_sample': nrm(ks[1], (DEC_BATCH, DEC_SEQ, D_MODEL), 1.0),
        'state_conv': nrm(ks[2], (DEPTH, DEC_BATCH, CONV_W - 1, D_CONV), 0.5),
        'state_hgrn': nrm(ks[3], (DEPTH, DEC_BATCH, H_HGRN, DK, DV), 0.5),
        'lb_logits': nrm(ks[4], (DEPTH + 1, HK), 0.5),
        'ln1_g': 1.0 + nrm(ks[5], (DEPTH, D_MODEL), 0.02),
        'ln1_b': nrm(ks[6], (DEPTH, D_MODEL), 0.02),
        'ffn1_w_up': nrm(ks[7], (DEPTH, D_MODEL, 2 * D_FF), D_MODEL ** -0.5),
        'ffn1_w_down': nrm(ks[8], (DEPTH, D_FF, D_MODEL), BETA * D_FF ** -0.5),
        'ln2_g': 1.0 + nrm(ks[9], (DEPTH, D_MODEL), 0.02),
        'ln2_b': nrm(ks[10], (DEPTH, D_MODEL), 0.02),
        'w_in': nrm(ks[11], (DEPTH, D_MODEL, d_in), D_MODEL ** -0.5) * col_scale,
        'conv_w': nrm(ks[12], (DEPTH, CONV_W, D_CONV), CONV_W ** -0.5),
        'conv_norm_g': 1.0 + nrm(ks[13], (DEPTH, D_CONV), 0.02),
        'hgrn_norm_g': 1.0 + nrm(ks[14], (DEPTH, HV), 0.02),
        'w_o': nrm(ks[15], (DEPTH, D_MIX, D_MODEL), BETA * D_MIX ** -0.5),
        'ln3_g': 1.0 + nrm(ks[16], (DEPTH, D_MODEL), 0.02),
        'ln3_b': nrm(ks[17], (DEPTH, D_MODEL), 0.02),
        'ffn2_w_up': nrm(ks[18], (DEPTH, D_MODEL, 2 * D_FF), D_MODEL ** -0.5),
        'ffn2_w_down': nrm(ks[19], (DEPTH, D_FF, D_MODEL), BETA * D_FF ** -0.5),
    }


def reference(x_prompt, x_sample, state_conv, state_hgrn, lb_logits, ln1_g, ln1_b, ffn1_w_up, ffn1_w_down,
              ln2_g, ln2_b, w_in, conv_w, conv_norm_g, hgrn_norm_g, w_o, ln3_g, ln3_b, ffn2_w_up, ffn2_w_down):
    lb_all = jnp.cumsum(jax.nn.softmax(lb_logits.astype(jnp.float32), axis=0), axis=0)
    yp, ys = x_prompt, x_sample
    conv_p, hgrn_p, conv_s, hgrn_s = [], [], [], []
    for l in range(DEPTH):
        params = (ln1_g[l], ln1_b[l], ffn1_w_up[l], ffn1_w_down[l], ln2_g[l], ln2_b[l], w_in[l], conv_w[l],
                  conv_norm_g[l], hgrn_norm_g[l], w_o[l], ln3_g[l], ln3_b[l], ffn2_w_up[l], ffn2_w_down[l])
        zero_conv = jnp.zeros((yp.shape[0], CONV_W - 1, D_CONV), yp.dtype)
        zero_hgrn = jnp.zeros((yp.shape[0], H_HGRN, DK, DV), jnp.float32)
        yp, cp, sp = decoder_layer(yp, zero_conv, zero_hgrn, lb_all[l], *params)
        ys, cs, ss = decoder_layer(ys, state_conv[l], state_hgrn[l], lb_all[l], *params)
        conv_p.append(cp)
        hgrn_p.append(sp)
        conv_s.append(cs)
        hgrn_s.append(ss)
    return (yp, ys, jnp.stack(conv_p), jnp.stack(hgrn_p), jnp.stack(conv_s), jnp.stack(hgrn_s))
```

```python
import functools
import math

import jax
import jax.numpy as jnp
from jax import lax
from jax.experimental import pallas as pl
from jax.experimental.pallas import tpu as pltpu

D_MODEL = 1024
D_CONV = 512
N_CONV_GROUPS = 8
CONV_W = 3
H_HGRN = 4
DK = 128
DV = 128
HK = H_HGRN * DK
D_FF = 2816
ALPHA = 2.0 ** 0.25
EPS = 1e-5

SUBLANES = 8
MXU_COLS = 256
VMEM_LIMIT = 56 * 1024 * 1024

FF_TILE = MXU_COLS
FFN_ROWS = 512
PROMPT_CHUNK = 256
SAMPLE_SEQS = 16

F32 = jnp.float32
BF16 = jnp.bfloat16


def _dot(a, b):
    return jnp.dot(a, b, preferred_element_type=F32)


def _dot_nt(a, b):
    return lax.dot_general(a, b, (((1,), (1,)), ((), ())), preferred_element_type=F32)


def _dot_tn(a, b):
    return lax.dot_general(a, b, (((0,), (0,)), ((), ())), preferred_element_type=F32)


def _sigmoid(x):
    return 1.0 / (1.0 + jnp.exp(-x))


def _layer_norm(z, g, b):
    mu = jnp.mean(z, axis=-1, keepdims=True)
    d = z - mu
    var = jnp.mean(d * d, axis=-1, keepdims=True)
    return d * lax.rsqrt(var + EPS) * g + b


def _const_spec(shape):
    nd = len(shape)
    return pl.BlockSpec(shape, lambda *_: (0,) * nd, pipeline_mode=pl.Buffered(1))


def _ffn_ln_kernel(x_ref, wup_ref, wdn_ref, g_ref, b_ref, o_ref):
    x = x_ref[...]
    xb = x.astype(BF16)
    acc = jnp.zeros(x.shape, F32)
    for j in range(D_FF // FF_TILE):
        c = j * FF_TILE
        a = _dot(xb, wup_ref[:, c:c + FF_TILE])
        b = _dot(xb, wup_ref[:, D_FF + c:D_FF + c + FF_TILE])
        h = (a * _sigmoid(a) * b).astype(BF16)
        acc = acc + _dot(h, wdn_ref[c:c + FF_TILE, :])
    o_ref[...] = _layer_norm(ALPHA * x + 0.5 * acc, g_ref[...], b_ref[...])


def _ffn_ln(x, wup, wdn, g, b):
    m = x.shape[0]
    tm = min(FFN_ROWS, m)
    assert m % tm == 0
    return pl.pallas_call(
        _ffn_ln_kernel,
        grid=(m // tm,),
        in_specs=[
            pl.BlockSpec((tm, D_MODEL), lambda i: (i, 0)),
            _const_spec(wup.shape),
            _const_spec(wdn.shape),
            _const_spec(g.shape),
            _const_spec(b.shape),
        ],
        out_specs=pl.BlockSpec((tm, D_MODEL), lambda i: (i, 0)),
        out_shape=jax.ShapeDtypeStruct((m, D_MODEL), F32),
        compiler_params=pltpu.CompilerParams(
            dimension_semantics=("parallel",), vmem_limit_bytes=VMEM_LIMIT),
    )(x, wup, wdn, g, b)


def _mixer_kernel(x_ref, cs_ref, hs_ref, lbl_ref, win_ref, cw_ref, cng_ref, hng_ref, gm_ref,
                  wo_ref, g_ref, b_ref, o_ref, cso_ref, hso_ref, st_ref, carry_ref,
                  *, n_seq, t_len, valid, n_tiles):
    j = pl.program_id(1)
    rows = n_seq * t_len
    log2_c = int(math.log2(t_len))
    assert 1 << log2_c == t_len and t_len % SUBLANES == 0
    assert valid == t_len or n_tiles == 1

    @pl.when(j == 0)
    def _():
        carry_ref[...] = cs_ref[...].reshape(n_seq * SUBLANES, D_CONV)
        for g in range(n_seq):
            for hd in range(H_HGRN):
                st_ref[g, hd] = hs_ref[g, hd].T

    x = x_ref[...].reshape(rows, D_MODEL)
    xb = x.astype(BF16)

    def proj(k):
        return _dot(xb, win_ref[:, k * D_CONV:(k + 1) * D_CONV])

    row_in_seq = lax.broadcasted_iota(jnp.int32, (rows, 1), 0) % t_len

    u = proj(1) * proj(0)
    prev = carry_ref[...]
    r1 = pltpu.roll(u, 1, 0)
    r2 = pltpu.roll(u, 2, 0)
    if t_len == SUBLANES:
        p1 = pltpu.roll(prev, rows - 1, 0)
        s1 = jnp.where(row_in_seq < 1, p1, r1)
        s2 = jnp.where(row_in_seq < 2, prev, r2)
        last8 = u
    else:
        assert n_seq == 1
        head = row_in_seq[:SUBLANES]
        t1 = jnp.where(head < 1, pltpu.roll(prev, SUBLANES - 1, 0), r1[:SUBLANES])
        t2 = jnp.where(head < 2, prev, r2[:SUBLANES])
        s1 = jnp.concatenate([t1, r1[SUBLANES:]], axis=0)
        s2 = jnp.concatenate([t2, r2[SUBLANES:]], axis=0)
        last8 = u[rows - SUBLANES:]
    n8 = n_seq * SUBLANES
    valid8 = valid - (t_len - SUBLANES)
    assert valid8 >= CONV_W - 1
    carry_ref[...] = pltpu.roll(last8, (n8 - (valid8 - 2)) % n8, 0)

    cw = cw_ref[...]
    conv = s2 * cw[0:1] + s1 * cw[1:2] + u * cw[2:3]
    z = proj(2) * conv
    zz = z * z
    zz_hi = zz.astype(BF16)
    zz_lo = (zz - zz_hi.astype(F32)).astype(BF16)
    gm = gm_ref[...]
    ms = (_dot(zz_hi, gm) + _dot(zz_lo, gm)) * (N_CONV_GROUPS / D_CONV)
    y_conv = z * lax.rsqrt(ms + EPS) * cng_ref[...]

    lbl = lbl_ref[...]
    lmax = jnp.max(lbl, axis=0, keepdims=True)
    le = jnp.exp(lbl - lmax)
    lb = le[0:1] / jnp.sum(le, axis=0, keepdims=True)
    q = proj(3)
    f = lb + (1.0 - lb) * _sigmoid(proj(4))
    logf = jnp.log(f)
    kk = 1.0 - f
    qf = q * _sigmoid(q)
    v = proj(5)
    gate_in = proj(6)
    gate = gate_in * _sigmoid(gate_in)
    if valid < t_len:
        live = row_in_seq < valid
        logf = jnp.where(live, logf, 0.0)
        kk = jnp.where(live, kk, 0.0)

    c, tot = logf, logf
    q_lvl, k_lvl = [], []
    for lh in range(log2_c):
        h = 1 << lh
        q_lvl.append((qf * jnp.exp(c)).astype(BF16))
        k_lvl.append((kk * jnp.exp(tot - c)).astype(BF16))
        later = ((row_in_seq >> lh) & 1) == 1
        before = pltpu.roll(tot, h, 0)
        after = pltpu.roll(tot, rows - h, 0)
        c = c + jnp.where(later, before, 0.0)
        tot = tot + jnp.where(later, before, after)
    q_chunk = (qf * jnp.exp(c)).astype(BF16)
    k_chunk = (kk * jnp.exp(tot - c)).astype(BF16)
    st_decay = jnp.exp(tot)
    qf_b, kk_b, v_b = qf.astype(BF16), kk.astype(BF16), v.astype(BF16)

    ti = lax.broadcasted_iota(jnp.int32, (rows, rows), 0)
    si = lax.broadcasted_iota(jnp.int32, (rows, rows), 1)
    pair = jnp.where(ti > si, ti ^ si, 0)
    on_diag = ti == si

    hng = hng_ref[...]
    y_heads = []
    for hd in range(H_HGRN):
        sl = slice(hd * DK, (hd + 1) * DK)
        scores = jnp.where(on_diag, _dot_nt(qf_b[:, sl], kk_b[:, sl]), 0.0)
        for lh in range(log2_c):
            p = _dot_nt(q_lvl[lh][:, sl], k_lvl[lh][:, sl])
            scores = jnp.where((pair >> lh) == 1, p, scores)
        o = _dot(scores.astype(BF16), v_b[:, sl])
        inter = []
        for g in range(n_seq):
            rs = slice(g * t_len, (g + 1) * t_len)
            st = st_ref[g, hd]
            inter.append(_dot_nt(q_chunk[rs, sl], st.astype(BF16)))
            st_ref[g, hd] = st * st_decay[g * t_len:g * t_len + 1, sl] + _dot_tn(v_b[rs, sl], k_chunk[rs, sl])
        o = o + (inter[0] if n_seq == 1 else jnp.concatenate(inter, axis=0))
        ms_h = jnp.mean(o * o, axis=-1, keepdims=True)
        y_heads.append(o * lax.rsqrt(ms_h + EPS) * hng[:, sl] * gate[:, sl])

    y = jnp.concatenate([y_conv] + y_heads, axis=-1).astype(BF16)
    mixed = _dot(y, wo_ref[...])
    out = _layer_norm(ALPHA * x + mixed, g_ref[...], b_ref[...])
    o_ref[...] = out.reshape(n_seq, t_len, D_MODEL)

    @pl.when(j == n_tiles - 1)
    def _():
        cso_ref[...] = carry_ref[...].reshape(n_seq, SUBLANES, D_CONV)
        for g in range(n_seq):
            for hd in range(H_HGRN):
                hso_ref[g, hd] = st_ref[g, hd].T


def _mixer_ln(x, conv_state, hgrn_state, valid, n_seq, t_len, lbl, win, cw, cng, hng, gm, wo, g, b):
    bsz, seq_len, _ = x.shape
    assert bsz % n_seq == 0 and seq_len % t_len == 0
    n_tiles = seq_len // t_len
    kern = functools.partial(_mixer_kernel, n_seq=n_seq, t_len=t_len,
                             valid=valid - (seq_len - t_len), n_tiles=n_tiles)
    return pl.pallas_call(
        kern,
        grid=(bsz // n_seq, n_tiles),
        in_specs=[
            pl.BlockSpec((n_seq, t_len, D_MODEL), lambda i, j: (i, j, 0)),
            pl.BlockSpec((n_seq, SUBLANES, D_CONV), lambda i, j: (i, 0, 0)),
            pl.BlockSpec((n_seq, H_HGRN, DK, DV), lambda i, j: (i, 0, 0, 0)),
            _const_spec(lbl.shape), _const_spec(win.shape), _const_spec(cw.shape),
            _const_spec(cng.shape), _const_spec(hng.shape), _const_spec(gm.shape),
            _const_spec(wo.shape), _const_spec(g.shape), _const_spec(b.shape),
        ],
        out_specs=[
            pl.BlockSpec((n_seq, t_len, D_MODEL), lambda i, j: (i, j, 0)),
            pl.BlockSpec((n_seq, SUBLANES, D_CONV), lambda i, j: (i, 0, 0)),
            pl.BlockSpec((n_seq, H_HGRN, DK, DV), lambda i, j: (i, 0, 0, 0)),
        ],
        out_shape=[
            jax.ShapeDtypeStruct(x.shape, F32),
            jax.ShapeDtypeStruct((bsz, SUBLANES, D_CONV), F32),
            jax.ShapeDtypeStruct((bsz, H_HGRN, DK, DV), F32),
        ],
        scratch_shapes=[
            pltpu.VMEM((n_seq, H_HGRN, DV, DK), F32),
            pltpu.VMEM((n_seq * SUBLANES, D_CONV), F32),
        ],
        compiler_params=pltpu.CompilerParams(
            dimension_semantics=("parallel", "arbitrary"), vmem_limit_bytes=VMEM_LIMIT),
    )(x, conv_state, hgrn_state, lbl, win, cw, cng, hng, gm, wo, g, b)


def kernel(x_prompt, x_sample, state_conv, state_hgrn, lb_logits, ln1_g, ln1_b, ffn1_w_up, ffn1_w_down,
           ln2_g, ln2_b, w_in, conv_w, conv_norm_g, hgrn_norm_g, w_o, ln3_g, ln3_b, ffn2_w_up, ffn2_w_down):
    bsz, seq_len, _ = x_prompt.shape
    dec_b, dec_l, _ = x_sample.shape
    l = 0
    row = lambda a: a[l].reshape(1, -1)
    wup1, wdn1 = ffn1_w_up[l].astype(BF16), ffn1_w_down[l].astype(BF16)
    wup2, wdn2 = ffn2_w_up[l].astype(BF16), ffn2_w_down[l].astype(BF16)
    win, wo = w_in[l].astype(BF16), w_o[l].astype(BF16)
    lane_group = jnp.arange(D_CONV) // (D_CONV // N_CONV_GROUPS)
    gm = (lane_group[:, None] == lane_group[None, :]).astype(BF16)
    mix_w = (lb_logits, win, conv_w[l], row(conv_norm_g), row(hgrn_norm_g), gm, wo, row(ln2_g), row(ln2_b))

    def pad_conv(cs):
        return jnp.pad(cs, ((0, 0), (0, SUBLANES - (CONV_W - 1)), (0, 0)))

    xp = _ffn_ln(x_prompt.reshape(bsz * seq_len, D_MODEL), wup1, wdn1, row(ln1_g), row(ln1_b))
    zero_conv = jnp.zeros((bsz, SUBLANES, D_CONV), F32)
    zero_hgrn = jnp.zeros((bsz, H_HGRN, DK, DV), F32)
    xp, conv_p, hgrn_p = _mixer_ln(xp.reshape(bsz, seq_len, D_MODEL), zero_conv, zero_hgrn, seq_len,
                                   1, PROMPT_CHUNK, *mix_w)
    yp = _ffn_ln(xp.reshape(bsz * seq_len, D_MODEL), wup2, wdn2, row(ln3_g), row(ln3_b))

    xs = _ffn_ln(x_sample.reshape(dec_b * dec_l, D_MODEL), wup1, wdn1, row(ln1_g), row(ln1_b))
    xs = jnp.pad(xs.reshape(dec_b, dec_l, D_MODEL), ((0, 0), (0, SUBLANES - dec_l), (0, 0)))
    xs, conv_s, hgrn_s = _mixer_ln(xs, pad_conv(state_conv[l]), state_hgrn[l], dec_l,
                                   SAMPLE_SEQS, SUBLANES, *mix_w)
    ys = _ffn_ln(xs[:, :dec_l].reshape(dec_b * dec_l, D_MODEL), wup2, wdn2, row(ln3_g), row(ln3_b))

    return (yp.reshape(bsz, seq_len, D_MODEL), ys.reshape(dec_b, dec_l, D_MODEL),
            conv_p[None, :, :CONV_W - 1], hgrn_p[None], conv_s[None, :, :CONV_W - 1], hgrn_s[None])
```

```python
import functools
import math

import jax
import jax.numpy as jnp
from jax import lax
from jax.experimental import pallas as pl
from jax.experimental.pallas import tpu as pltpu

D_MODEL = 1024
D_CONV = 512
N_CONV_GROUPS = 8
CONV_W = 3
H_HGRN = 4
DK = 128
DV = 128
HK = H_HGRN * DK
D_FF = 2816
ALPHA = 2.0 ** 0.25
EPS = 1e-5

SUBLANES = 8
MXU_COLS = 256
VMEM_LIMIT = 56 * 1024 * 1024

SCORE_TILE = 128
LOG2_TILE = 7
FF_TILE = MXU_COLS
FFN_ROWS = 512
PROMPT_CHUNK = 256
SAMPLE_SEQS = 16

F32 = jnp.float32
BF16 = jnp.bfloat16


def _dot(a, b):
    return jnp.dot(a, b, preferred_element_type=F32)


def _dot_nt(a, b):
    return lax.dot_general(a, b, (((1,), (1,)), ((), ())), preferred_element_type=F32)


def _dot_tn(a, b):
    return lax.dot_general(a, b, (((0,), (0,)), ((), ())), preferred_element_type=F32)


def _sigmoid(x):
    return 1.0 / (1.0 + jnp.exp(-x))


def _layer_norm(z, g, b):
    mu = jnp.mean(z, axis=-1, keepdims=True)
    d = z - mu
    var = jnp.mean(d * d, axis=-1, keepdims=True)
    return d * lax.rsqrt(var + EPS) * g + b


def _const_spec(shape):
    nd = len(shape)
    return pl.BlockSpec(shape, lambda *_: (0,) * nd, pipeline_mode=pl.Buffered(1))


def _ffn_ln_kernel(x_ref, wup_ref, wdn_ref, g_ref, b_ref, o_ref):
    x = x_ref[...]
    xb = x.astype(BF16)
    acc = jnp.zeros(x.shape, F32)
    for j in range(D_FF // FF_TILE):
        c = j * FF_TILE
        a = _dot(xb, wup_ref[:, c:c + FF_TILE])
        b = _dot(xb, wup_ref[:, D_FF + c:D_FF + c + FF_TILE])
        h = (a * _sigmoid(a) * b).astype(BF16)
        acc = acc + _dot(h, wdn_ref[c:c + FF_TILE, :])
    o_ref[...] = _layer_norm(ALPHA * x + 0.5 * acc, g_ref[...], b_ref[...])


def _ffn_ln(x, wup, wdn, g, b):
    m = x.shape[0]
    tm = min(FFN_ROWS, m)
    assert m % tm == 0
    return pl.pallas_call(
        _ffn_ln_kernel,
        grid=(m // tm,),
        in_specs=[
            pl.BlockSpec((tm, D_MODEL), lambda i: (i, 0)),
            _const_spec(wup.shape),
            _const_spec(wdn.shape),
            _const_spec(g.shape),
            _const_spec(b.shape),
        ],
        out_specs=pl.BlockSpec((tm, D_MODEL), lambda i: (i, 0)),
        out_shape=jax.ShapeDtypeStruct((m, D_MODEL), F32),
        compiler_params=pltpu.CompilerParams(
            dimension_semantics=("parallel",), vmem_limit_bytes=VMEM_LIMIT),
    )(x, wup, wdn, g, b)


def _mixer_kernel(x_ref, cs_ref, hs_ref, lbl_ref, win_ref, cw_ref, cng_ref, hng_ref, gm_ref,
                  wo_ref, g_ref, b_ref, o_ref, cso_ref, hso_ref, st_ref, carry_ref,
                  *, n_seq, t_len, valid, n_tiles):
    j = pl.program_id(1)
    rows = n_seq * t_len
    log2_c = int(math.log2(t_len))
    assert 1 << log2_c == t_len and t_len % SUBLANES == 0
    assert valid == t_len or n_tiles == 1

    @pl.when(j == 0)
    def _():
        carry_ref[...] = cs_ref[...].reshape(n_seq * SUBLANES, D_CONV)
        for g in range(n_seq):
            for hd in range(H_HGRN):
                st_ref[g, hd] = hs_ref[g, hd].T

    x = x_ref[...].reshape(rows, D_MODEL)
    xb = x.astype(BF16)

    def proj(k):
        return _dot(xb, win_ref[:, k * D_CONV:(k + 1) * D_CONV])

    row_in_seq = lax.broadcasted_iota(jnp.int32, (rows, 1), 0) % t_len

    u = proj(1) * proj(0)
    prev = carry_ref[...]
    r1 = pltpu.roll(u, 1, 0)
    r2 = pltpu.roll(u, 2, 0)
    if t_len == SUBLANES:
        p1 = pltpu.roll(prev, rows - 1, 0)
        s1 = jnp.where(row_in_seq < 1, p1, r1)
        s2 = jnp.where(row_in_seq < 2, prev, r2)
        last8 = u
    else:
        assert n_seq == 1
        head = row_in_seq[:SUBLANES]
        t1 = jnp.where(head < 1, pltpu.roll(prev, SUBLANES - 1, 0), r1[:SUBLANES])
        t2 = jnp.where(head < 2, prev, r2[:SUBLANES])
        s1 = jnp.concatenate([t1, r1[SUBLANES:]], axis=0)
        s2 = jnp.concatenate([t2, r2[SUBLANES:]], axis=0)
        last8 = u[rows - SUBLANES:]
    n8 = n_seq * SUBLANES
    valid8 = valid - (t_len - SUBLANES)
    assert valid8 >= CONV_W - 1
    carry_ref[...] = pltpu.roll(last8, (n8 - (valid8 - 2)) % n8, 0)

    cw = cw_ref[...]
    conv = s2 * cw[0:1] + s1 * cw[1:2] + u * cw[2:3]
    z = proj(2) * conv
    zz = z * z
    zz_hi = zz.astype(BF16)
    zz_lo = (zz - zz_hi.astype(F32)).astype(BF16)
    gm = gm_ref[...]
    ms = (_dot(zz_hi, gm) + _dot(zz_lo, gm)) * (N_CONV_GROUPS / D_CONV)
    y_conv = z * lax.rsqrt(ms + EPS) * cng_ref[...]

    lbl = lbl_ref[...]
    lmax = jnp.max(lbl, axis=0, keepdims=True)
    le = jnp.exp(lbl - lmax)
    lb = le[0:1] / jnp.sum(le, axis=0, keepdims=True)
    q = proj(3)
    f = lb + (1.0 - lb) * _sigmoid(proj(4))
    kk = 1.0 - f
    qf = q * _sigmoid(q)
    v = proj(5)
    gate_in = proj(6)
    gate = gate_in * _sigmoid(gate_in)
    if valid < t_len:
        live = row_in_seq < valid
        f = jnp.where(live, f, 1.0)
        kk = jnp.where(live, kk, 0.0)

    q_pre, k_post, blk = qf * f, kk, f
    q_lvl, k_lvl = [qf.astype(BF16)], [kk.astype(BF16)]
    n_levels = min(log2_c, LOG2_TILE)
    for lh in range(log2_c):
        h = 1 << lh
        if lh < n_levels:
            q_lvl.append(q_pre.astype(BF16))
            k_lvl.append(k_post.astype(BF16))
        if h < SUBLANES:
            later = ((row_in_seq >> lh) & 1) == 1
            before = pltpu.roll(blk, h, 0)
            after = pltpu.roll(blk, rows - h, 0)
            q_pre = q_pre * jnp.where(later, before, 1.0)
            k_post = k_post * jnp.where(later, 1.0, after)
            blk = blk * jnp.where(later, before, after)
        else:
            halves = lambda a: jnp.split(a.reshape(rows // (2 * h), 2, h, HK), 2, axis=1)
            join = lambda lo, hi: jnp.concatenate([lo, hi], axis=1).reshape(rows, HK)
            (q_lo, q_hi), (k_lo, k_hi), (b_lo, b_hi) = halves(q_pre), halves(k_post), halves(blk)
            if lh >= n_levels:
                q_top, k_top = q_hi.reshape(h, HK).astype(BF16), k_lo.reshape(h, HK).astype(BF16)
            q_pre = join(q_lo, q_hi * b_lo)
            k_post = join(k_lo * b_hi, k_hi)
            blk = join(b_lo * b_hi, b_lo * b_hi)
    q_chunk = q_pre.astype(BF16)
    k_chunk = k_post.astype(BF16)
    v_b = v.astype(BF16)

    ti = lax.broadcasted_iota(jnp.int32, (SCORE_TILE, SCORE_TILE), 0)
    si = lax.broadcasted_iota(jnp.int32, (SCORE_TILE, SCORE_TILE), 1)
    txs = ti ^ si
    pair_level = jnp.where(ti > si, 0, jnp.where(ti == si, -1, -2))
    for lh in range(1, LOG2_TILE):
        pair_level = pair_level + jnp.where((ti > si) & (txs >= (1 << lh)), 1, 0)

    n_tile = rows // SCORE_TILE
    assert t_len <= SCORE_TILE or (n_seq == 1 and t_len == 2 * SCORE_TILE)
    hng = hng_ref[...]
    y_heads = []
    for hd in range(H_HGRN):
        sl = slice(hd * DK, (hd + 1) * DK)
        tiles = []
        for it in range(n_tile):
            rt = slice(it * SCORE_TILE, (it + 1) * SCORE_TILE)
            sc = jnp.zeros((SCORE_TILE, SCORE_TILE), F32)
            for lv in range(n_levels + 1):
                p = _dot_nt(q_lvl[lv][rt, sl], k_lvl[lv][rt, sl])
                sc = jnp.where(pair_level == lv - 1, p, sc)
            tiles.append(sc.astype(BF16))
        if n_tile == 1:
            o = _dot(tiles[0], v_b[:, sl])
        elif t_len <= SCORE_TILE:
            o = jnp.concatenate([_dot(tiles[it], v_b[it * SCORE_TILE:(it + 1) * SCORE_TILE, sl])
                                 for it in range(n_tile)], axis=0)
        else:
            cross = _dot_nt(q_top[:, sl], k_top[:, sl]).astype(BF16)
            o = jnp.concatenate([_dot(tiles[0], v_b[:SCORE_TILE, sl]),
                                 _dot(jnp.concatenate([cross, tiles[1]], axis=1), v_b[:, sl])], axis=0)
        inter = []
        for g in range(n_seq):
            rs = slice(g * t_len, (g + 1) * t_len)
            st = st_ref[g, hd]
            inter.append(_dot_nt(q_chunk[rs, sl], st.astype(BF16)))
            st_ref[g, hd] = st * blk[g * t_len:g * t_len + 1, sl] + _dot_tn(v_b[rs, sl], k_chunk[rs, sl])
        o = o + (inter[0] if n_seq == 1 else jnp.concatenate(inter, axis=0))
        ms_h = jnp.mean(o * o, axis=-1, keepdims=True)
        y_heads.append(o * lax.rsqrt(ms_h + EPS) * hng[:, sl] * gate[:, sl])

    y = jnp.concatenate([y_conv] + y_heads, axis=-1).astype(BF16)
    mixed = _dot(y, wo_ref[...])
    out = _layer_norm(ALPHA * x + mixed, g_ref[...], b_ref[...])
    o_ref[...] = out.reshape(n_seq, t_len, D_MODEL)

    @pl.when(j == n_tiles - 1)
    def _():
        cso_ref[...] = carry_ref[...].reshape(n_seq, SUBLANES, D_CONV)
        for g in range(n_seq):
            for hd in range(H_HGRN):
                hso_ref[g, hd] = st_ref[g, hd].T


def _mixer_ln(x, conv_state, hgrn_state, valid, n_seq, t_len, lbl, win, cw, cng, hng, gm, wo, g, b):
    bsz, seq_len, _ = x.shape
    assert bsz % n_seq == 0 and seq_len % t_len == 0
    n_tiles = seq_len // t_len
    kern = functools.partial(_mixer_kernel, n_seq=n_seq, t_len=t_len,
                             valid=valid - (seq_len - t_len), n_tiles=n_tiles)
    return pl.pallas_call(
        kern,
        grid=(bsz // n_seq, n_tiles),
        in_specs=[
            pl.BlockSpec((n_seq, t_len, D_MODEL), lambda i, j: (i, j, 0)),
            pl.BlockSpec((n_seq, SUBLANES, D_CONV), lambda i, j: (i, 0, 0)),
            pl.BlockSpec((n_seq, H_HGRN, DK, DV), lambda i, j: (i, 0, 0, 0)),
            _const_spec(lbl.shape), _const_spec(win.shape), _const_spec(cw.shape),
            _const_spec(cng.shape), _const_spec(hng.shape), _const_spec(gm.shape),
            _const_spec(wo.shape), _const_spec(g.shape), _const_spec(b.shape),
        ],
        out_specs=[
            pl.BlockSpec((n_seq, t_len, D_MODEL), lambda i, j: (i, j, 0)),
            pl.BlockSpec((n_seq, SUBLANES, D_CONV), lambda i, j: (i, 0, 0)),
            pl.BlockSpec((n_seq, H_HGRN, DK, DV), lambda i, j: (i, 0, 0, 0)),
        ],
        out_shape=[
            jax.ShapeDtypeStruct(x.shape, F32),
            jax.ShapeDtypeStruct((bsz, SUBLANES, D_CONV), F32),
            jax.ShapeDtypeStruct((bsz, H_HGRN, DK, DV), F32),
        ],
        scratch_shapes=[
            pltpu.VMEM((n_seq, H_HGRN, DV, DK), F32),
            pltpu.VMEM((n_seq * SUBLANES, D_CONV), F32),
        ],
        compiler_params=pltpu.CompilerParams(
            dimension_semantics=("parallel", "arbitrary"), vmem_limit_bytes=VMEM_LIMIT),
    )(x, conv_state, hgrn_state, lbl, win, cw, cng, hng, gm, wo, g, b)


def kernel(x_prompt, x_sample, state_conv, state_hgrn, lb_logits, ln1_g, ln1_b, ffn1_w_up, ffn1_w_down,
           ln2_g, ln2_b, w_in, conv_w, conv_norm_g, hgrn_norm_g, w_o, ln3_g, ln3_b, ffn2_w_up, ffn2_w_down):
    bsz, seq_len, _ = x_prompt.shape
    dec_b, dec_l, _ = x_sample.shape
    l = 0
    row = lambda a: a[l].reshape(1, -1)
    wup1, wdn1 = ffn1_w_up[l].astype(BF16), ffn1_w_down[l].astype(BF16)
    wup2, wdn2 = ffn2_w_up[l].astype(BF16), ffn2_w_down[l].astype(BF16)
    win, wo = w_in[l].astype(BF16), w_o[l].astype(BF16)
    lane_group = jnp.arange(D_CONV) // (D_CONV // N_CONV_GROUPS)
    gm = (lane_group[:, None] == lane_group[None, :]).astype(BF16)
    mix_w = (lb_logits, win, conv_w[l], row(conv_norm_g), row(hgrn_norm_g), gm, wo, row(ln2_g), row(ln2_b))

    def pad_conv(cs):
        return jnp.pad(cs, ((0, 0), (0, SUBLANES - (CONV_W - 1)), (0, 0)))

    xp = _ffn_ln(x_prompt.reshape(bsz * seq_len, D_MODEL), wup1, wdn1, row(ln1_g), row(ln1_b))
    zero_conv = jnp.zeros((bsz, SUBLANES, D_CONV), F32)
    zero_hgrn = jnp.zeros((bsz, H_HGRN, DK, DV), F32)
    xp, conv_p, hgrn_p = _mixer_ln(xp.reshape(bsz, seq_len, D_MODEL), zero_conv, zero_hgrn, seq_len,
                                   1, PROMPT_CHUNK, *mix_w)
    yp = _ffn_ln(xp.reshape(bsz * seq_len, D_MODEL), wup2, wdn2, row(ln3_g), row(ln3_b))

    xs = _ffn_ln(x_sample.reshape(dec_b * dec_l, D_MODEL), wup1, wdn1, row(ln1_g), row(ln1_b))
    xs = jnp.pad(xs.reshape(dec_b, dec_l, D_MODEL), ((0, 0), (0, SUBLANES - dec_l), (0, 0)))
    xs, conv_s, hgrn_s = _mixer_ln(xs, pad_conv(state_conv[l]), state_hgrn[l], dec_l,
                                   SAMPLE_SEQS, SUBLANES, *mix_w)
    ys = _ffn_ln(xs[:, :dec_l].reshape(dec_b * dec_l, D_MODEL), wup2, wdn2, row(ln3_g), row(ln3_b))

    return (yp.reshape(bsz, seq_len, D_MODEL), ys.reshape(dec_b, dec_l, D_MODEL),
            conv_p[None, :, :CONV_W - 1], hgrn_p[None], conv_s[None, :, :CONV_W - 1], hgrn_s[None])
```

```python
import functools
import math
import types

import jax
import jax.numpy as jnp
from jax import lax
from jax.experimental import pallas as pl
from jax.experimental.pallas import tpu as pltpu

D_MODEL = 1024
D_CONV = 512
N_CONV_GROUPS = 8
CONV_W = 3
H_HGRN = 4
DK = 128
DV = 128
HK = H_HGRN * DK
D_FF = 2816
ALPHA = 2.0 ** 0.25
EPS = 1e-5

SUBLANES = 8
MXU_COLS = 256
VMEM_LIMIT = 56 * 1024 * 1024

SCORE_TILE = 128
LOG2_TILE = 7
FF_TILE = MXU_COLS
FFN_ROWS = 512
PROMPT_CHUNK = 256
SAMPLE_SEQS = 16

F32 = jnp.float32
BF16 = jnp.bfloat16


def _dot(a, b):
    return jnp.dot(a, b, preferred_element_type=F32)


def _dot_nt(a, b):
    return lax.dot_general(a, b, (((1,), (1,)), ((), ())), preferred_element_type=F32)


def _dot_tn(a, b):
    return lax.dot_general(a, b, (((0,), (0,)), ((), ())), preferred_element_type=F32)


def _sigmoid(x):
    return 0.5 * jnp.tanh(0.5 * x) + 0.5


def _layer_norm(z, g, b):
    mu = jnp.mean(z, axis=-1, keepdims=True)
    d = z - mu
    var = jnp.mean(d * d, axis=-1, keepdims=True)
    return d * lax.rsqrt(var + EPS) * g + b


def _const_spec(shape):
    nd = len(shape)
    return pl.BlockSpec(shape, lambda *_: (0,) * nd, pipeline_mode=pl.Buffered(1))


def _ffn_ln_kernel(x_ref, wup_ref, wdn_ref, g_ref, b_ref, o_ref):
    x = x_ref[...]
    xb = x.astype(BF16)
    acc = jnp.zeros(x.shape, F32)
    for j in range(D_FF // FF_TILE):
        c = j * FF_TILE
        a = _dot(xb, wup_ref[:, c:c + FF_TILE])
        b = _dot(xb, wup_ref[:, D_FF + c:D_FF + c + FF_TILE])
        h = (a * _sigmoid(a) * b).astype(BF16)
        acc = acc + _dot(h, wdn_ref[c:c + FF_TILE, :])
    o_ref[...] = _layer_norm(ALPHA * x + 0.5 * acc, g_ref[...], b_ref[...])


def _ffn_ln(x, wup, wdn, g, b):
    m = x.shape[0]
    tm = min(FFN_ROWS, m)
    assert m % tm == 0
    return pl.pallas_call(
        _ffn_ln_kernel,
        grid=(m // tm,),
        in_specs=[
            pl.BlockSpec((tm, D_MODEL), lambda i: (i, 0)),
            _const_spec(wup.shape),
            _const_spec(wdn.shape),
            _const_spec(g.shape),
            _const_spec(b.shape),
        ],
        out_specs=pl.BlockSpec((tm, D_MODEL), lambda i: (i, 0)),
        out_shape=jax.ShapeDtypeStruct((m, D_MODEL), F32),
        compiler_params=pltpu.CompilerParams(
            dimension_semantics=("parallel",), vmem_limit_bytes=VMEM_LIMIT),
    )(x, wup, wdn, g, b)


class _Levels:
    def __init__(self, arrays):
        self.arrays = arrays

    def __getitem__(self, idx):
        return self.arrays[idx[0]][idx[1:]]


def _n_levels(t_len):
    return min(int(math.log2(t_len)), LOG2_TILE)


def _interleave(*phases):
    results = [None] * len(phases)
    live = list(range(len(phases)))
    while live:
        for i in list(live):
            try:
                next(phases[i])
            except StopIteration as stop:
                results[i] = stop.value
                live.remove(i)
    return results


def _mixer_front(x, prev, lbl_ref, win_ref, cw_ref, cng_ref, gm_ref, *, n_seq, t_len, valid):
    rows = n_seq * t_len
    log2_c = int(math.log2(t_len))
    assert 1 << log2_c == t_len and t_len % SUBLANES == 0 and valid >= CONV_W - 1
    xb = x.astype(BF16)

    def proj(k):
        return _dot(xb, win_ref[:, k * D_CONV:(k + 1) * D_CONV])

    u = (proj(1) * proj(0)).reshape(n_seq, t_len, D_CONV)
    yield
    first = lax.broadcasted_iota(jnp.int32, (1, SUBLANES, 1), 1) == 0

    def delay(a, before):
        r = pltpu.roll(a, 1, 1)
        head = jnp.where(first, before, r[:, :SUBLANES])
        return head if t_len == SUBLANES else jnp.concatenate([head, r[:, SUBLANES:]], axis=1)

    cw = cw_ref[...]
    w0, w1, w2 = cw[0:1], cw[1:2], cw[2:3]
    pair = w0 * delay(u, prev[:, SUBLANES - 1:]) + w1 * u
    pair_before = w0 * prev[:, SUBLANES - 2:SUBLANES - 1] + w1 * prev[:, SUBLANES - 1:]
    conv = (delay(pair, pair_before) + w2 * u).reshape(rows, D_CONV)
    pad8 = t_len - valid
    assert pad8 < SUBLANES
    last8 = u[:, t_len - SUBLANES:]
    next_prev = last8 if pad8 == 0 else pltpu.roll(last8, pad8, 1)
    z = proj(2) * conv
    yield
    zz = z * z
    zz_hi = zz.astype(BF16)
    zz_lo = (zz - zz_hi.astype(F32)).astype(BF16)
    gm = gm_ref[...]
    ms = _dot(zz_hi, gm) + _dot(zz_lo, gm)
    y_conv = z * lax.rsqrt(ms + EPS) * cng_ref[...]

    lbl = lbl_ref[...]
    lmax = jnp.max(lbl, axis=0, keepdims=True)
    le = jnp.exp(lbl - lmax)
    lb = le[0:1] / jnp.sum(le, axis=0, keepdims=True)
    yield
    q = proj(3)
    qf = q * _sigmoid(q)
    yield
    f = lb + (1.0 - lb) * _sigmoid(proj(4))
    kk = 1.0 - f
    yield
    v = proj(5)
    yield
    gate_in = proj(6)
    gate = gate_in * _sigmoid(gate_in)
    yield
    if valid < t_len:
        live = lax.broadcasted_iota(jnp.int32, (rows, 1), 0) % t_len < valid
        f = jnp.where(live, f, 1.0)
        kk = jnp.where(live, kk, 0.0)

    q_pre, k_post, blk = qf * f, kk, f
    q_lvl, k_lvl = [qf.astype(BF16)], [kk.astype(BF16)]
    q_top = k_top = None
    n_levels = _n_levels(t_len)
    for lh in range(log2_c):
        h = 1 << lh
        if lh < n_levels:
            q_lvl.append(q_pre.astype(BF16))
            k_lvl.append(k_post.astype(BF16))
        if h < SUBLANES:
            groups = lambda a: a.reshape(rows // SUBLANES, SUBLANES, HK)
            later = ((lax.broadcasted_iota(jnp.int32, (1, SUBLANES, 1), 1) >> lh) & 1) == 1
            other = pltpu.roll(groups(blk), h, 1)
            if 2 * h < SUBLANES:
                other = jnp.where(later, other, pltpu.roll(groups(blk), SUBLANES - h, 1))
            q_pre = (groups(q_pre) * jnp.where(later, other, 1.0)).reshape(rows, HK)
            k_post = (groups(k_post) * jnp.where(later, 1.0, other)).reshape(rows, HK)
            blk = (groups(blk) * other).reshape(rows, HK)
        else:
            halves = lambda a: jnp.split(a.reshape(rows // (2 * h), 2, h, HK), 2, axis=1)
            join = lambda lo, hi: jnp.concatenate([lo, hi], axis=1).reshape(rows, HK)
            (q_lo, q_hi), (k_lo, k_hi), (b_lo, b_hi) = halves(q_pre), halves(k_post), halves(blk)
            if lh >= n_levels:
                q_top = q_hi.reshape(n_seq * h, HK).astype(BF16)
                k_top = k_lo.reshape(n_seq * h, HK).astype(BF16)
            q_pre = join(q_lo, q_hi * b_lo)
            k_post = join(k_lo * b_hi, k_hi)
            blk = join(b_lo * b_hi, b_lo * b_hi)
        if lh % 2 == 1:
            yield
    ops = types.SimpleNamespace(
        ql=_Levels(q_lvl), kl=_Levels(k_lvl), qt=q_top, kt=k_top,
        qc=q_pre.astype(BF16),
        kc=k_post.astype(BF16),
        vb=v.astype(BF16), blk=blk, gate=gate, yc=y_conv.astype(BF16))
    return next_prev, ops


def _mixer_back(ops, x, st_ref, hng_ref, wo_ref, g_ref, b_ref, *, n_seq, t_len):
    rows = n_seq * t_len
    n_levels = _n_levels(t_len)
    n_tile = rows // SCORE_TILE
    assert rows % SCORE_TILE == 0 and (t_len <= SCORE_TILE or t_len == 2 * SCORE_TILE)

    ti = lax.broadcasted_iota(jnp.int32, (SCORE_TILE, SCORE_TILE), 0)
    si = lax.broadcasted_iota(jnp.int32, (SCORE_TILE, SCORE_TILE), 1)
    txs = ti ^ si
    pair_level = jnp.where(ti > si, 0, jnp.where(ti == si, -1, -2))
    for lh in range(1, LOG2_TILE):
        pair_level = pair_level + jnp.where((ti > si) & (txs >= (1 << lh)), 1, 0)

    hng = hng_ref[...]
    y_heads = []
    for hd in range(H_HGRN):
        sl = slice(hd * DK, (hd + 1) * DK)
        tiles = []
        for it in range(n_tile):
            rt = slice(it * SCORE_TILE, (it + 1) * SCORE_TILE)
            sc = jnp.zeros((SCORE_TILE, SCORE_TILE), F32)
            for lv in range(n_levels + 1):
                p = _dot_nt(ops.ql[lv, rt, sl], ops.kl[lv, rt, sl])
                sc = jnp.where(pair_level == lv - 1, p, sc)
            tiles.append(sc.astype(BF16))
            yield
        o_parts = []
        if t_len <= SCORE_TILE:
            for it in range(n_tile):
                o_parts.append(_dot(tiles[it], ops.vb[it * SCORE_TILE:(it + 1) * SCORE_TILE, sl]))
        else:
            for g in range(n_seq):
                top = slice(g * SCORE_TILE, (g + 1) * SCORE_TILE)
                cross = _dot_nt(ops.qt[top, sl], ops.kt[top, sl]).astype(BF16)
                o_parts.append(_dot(tiles[2 * g], ops.vb[g * t_len:g * t_len + SCORE_TILE, sl]))
                o_parts.append(_dot(jnp.concatenate([cross, tiles[2 * g + 1]], axis=1),
                                    ops.vb[g * t_len:(g + 1) * t_len, sl]))
        o = o_parts[0] if len(o_parts) == 1 else jnp.concatenate(o_parts, axis=0)
        inter = []
        for g in range(n_seq):
            rs = slice(g * t_len, (g + 1) * t_len)
            st = st_ref[g, hd]
            inter.append(_dot_nt(ops.qc[rs, sl], st.astype(BF16)))
            st_ref[g, hd] = st * ops.blk[g * t_len:g * t_len + 1, sl] + _dot_tn(ops.vb[rs, sl], ops.kc[rs, sl])
        o = o + (inter[0] if n_seq == 1 else jnp.concatenate(inter, axis=0))
        ms_h = jnp.mean(o * o, axis=-1, keepdims=True)
        y_heads.append((o * lax.rsqrt(ms_h + EPS) * hng[:, sl] * ops.gate[:, sl]).astype(BF16))
        yield

    y = jnp.concatenate([ops.yc[...]] + y_heads, axis=-1)
    mixed = _dot(y, wo_ref[...])
    return _layer_norm(ALPHA * x + mixed, g_ref[...], b_ref[...])


def _mixer_kernel(x_ref, cs_ref, hs_ref, lbl_ref, win_ref, cw_ref, cng_ref, hng_ref, gm_ref,
                  wo_ref, g_ref, b_ref, o_ref, cso_ref, hso_ref, st_ref, *, n_seq, t_len, valid):
    for g in range(n_seq):
        for hd in range(H_HGRN):
            st_ref[g, hd] = hs_ref[g, hd].T
    x = x_ref[...].reshape(n_seq * t_len, D_MODEL)
    (next_prev, ops), = _interleave(_mixer_front(x, cs_ref[...], lbl_ref, win_ref, cw_ref, cng_ref, gm_ref,
                                                 n_seq=n_seq, t_len=t_len, valid=valid))
    out, = _interleave(_mixer_back(ops, x, st_ref, hng_ref, wo_ref, g_ref, b_ref, n_seq=n_seq, t_len=t_len))
    o_ref[...] = out.reshape(n_seq, t_len, D_MODEL)
    cso_ref[...] = next_prev
    for g in range(n_seq):
        for hd in range(H_HGRN):
            hso_ref[g, hd] = st_ref[g, hd].T


def _mixer_ln_single(x, conv_state, hgrn_state, valid, n_seq, lbl, win, cw, cng, hng, gm, wo, g, b):
    bsz, t_len, _ = x.shape
    assert bsz % n_seq == 0
    kern = functools.partial(_mixer_kernel, n_seq=n_seq, t_len=t_len, valid=valid)
    seq_block = lambda *tail: pl.BlockSpec((n_seq,) + tail, lambda i: (i,) + (0,) * len(tail))
    return pl.pallas_call(
        kern,
        grid=(bsz // n_seq,),
        in_specs=[
            seq_block(t_len, D_MODEL), seq_block(SUBLANES, D_CONV), seq_block(H_HGRN, DK, DV),
            _const_spec(lbl.shape), _const_spec(win.shape), _const_spec(cw.shape),
            _const_spec(cng.shape), _const_spec(hng.shape), _const_spec(gm.shape),
            _const_spec(wo.shape), _const_spec(g.shape), _const_spec(b.shape),
        ],
        out_specs=[seq_block(t_len, D_MODEL), seq_block(SUBLANES, D_CONV), seq_block(H_HGRN, DK, DV)],
        out_shape=[
            jax.ShapeDtypeStruct(x.shape, F32),
            jax.ShapeDtypeStruct((bsz, SUBLANES, D_CONV), F32),
            jax.ShapeDtypeStruct((bsz, H_HGRN, DK, DV), F32),
        ],
        scratch_shapes=[pltpu.VMEM((n_seq, H_HGRN, DV, DK), F32)],
        compiler_params=pltpu.CompilerParams(
            dimension_semantics=("parallel",), vmem_limit_bytes=VMEM_LIMIT),
    )(x, conv_state, hgrn_state, lbl, win, cw, cng, hng, gm, wo, g, b)


_BUF_FIELDS = ("ql", "kl", "qt", "kt", "qc", "kc", "vb", "blk", "gate", "yc")


def _mixer_pipe_kernel(xa_ref, xb_ref, cs_ref, hs_ref, lbl_ref, win_ref, cw_ref, cng_ref, hng_ref, gm_ref,
                       wo_ref, g_ref, b_ref, o_ref, cso_ref, hso_ref, st_ref, carry_ref, *bufs,
                       t_len, tiles_per_seq, n_tiles):
    s = pl.program_id(0)
    nf = len(_BUF_FIELDS)
    buf_sets = [types.SimpleNamespace(**dict(zip(_BUF_FIELDS, bufs[p * nf:(p + 1) * nf]))) for p in range(2)]
    front_tile = jnp.minimum(s, n_tiles - 1) % tiles_per_seq
    back_tile = jnp.maximum(s - 1, 0) % tiles_per_seq

    @pl.when(s == 0)
    def _():
        for buf in bufs:
            buf[...] = jnp.zeros(buf.shape, buf.dtype)
        carry_ref[...] = jnp.zeros(carry_ref.shape, F32)

    @pl.when(back_tile == 0)
    def _():
        for hd in range(H_HGRN):
            st_ref[0, hd] = hs_ref[0, hd].T

    def step(read, write):
        prev = jnp.where(front_tile == 0, cs_ref[...], carry_ref[...])
        out, (prev, ops) = _interleave(
            _mixer_back(read, xb_ref[0], st_ref, hng_ref, wo_ref, g_ref, b_ref, n_seq=1, t_len=t_len),
            _mixer_front(xa_ref[0], prev, lbl_ref, win_ref, cw_ref, cng_ref, gm_ref,
                         n_seq=1, t_len=t_len, valid=t_len))
        o_ref[0] = out
        for lv in range(_n_levels(t_len) + 1):
            write.ql[lv] = ops.ql[lv, :, :]
            write.kl[lv] = ops.kl[lv, :, :]
        for name in _BUF_FIELDS[2:]:
            getattr(write, name)[...] = getattr(ops, name)
        carry_ref[...] = prev
        cso_ref[...] = prev

    for parity in range(2):
        pl.when(s % 2 == parity)(functools.partial(step, buf_sets[1 - parity], buf_sets[parity]))

    @pl.when((back_tile == tiles_per_seq - 1) & (s > 0))
    def _():
        for hd in range(H_HGRN):
            hso_ref[0, hd] = st_ref[0, hd].T


def _mixer_ln_pipe(x, conv_state, hgrn_state, t_len, lbl, win, cw, cng, hng, gm, wo, g, b):
    bsz, seq_len, _ = x.shape
    assert seq_len % t_len == 0 and t_len == 2 * SCORE_TILE
    tiles_per_seq = seq_len // t_len
    n_tiles = bsz * tiles_per_seq
    kern = functools.partial(_mixer_pipe_kernel, t_len=t_len, tiles_per_seq=tiles_per_seq, n_tiles=n_tiles)
    front = lambda s: jnp.minimum(s, n_tiles - 1)
    back = lambda s: jnp.maximum(s - 1, 0)
    tile_of = lambda t: (t // tiles_per_seq, t % tiles_per_seq, 0)
    seq_of = lambda t, nd: (t // tiles_per_seq,) + (0,) * nd
    n_lv = _n_levels(t_len) + 1
    buf_shapes = dict(ql=((n_lv, t_len, HK), BF16), kl=((n_lv, t_len, HK), BF16),
                      qt=((t_len // 2, HK), BF16), kt=((t_len // 2, HK), BF16),
                      qc=((t_len, HK), BF16), kc=((t_len, HK), BF16), vb=((t_len, HK), BF16),
                      blk=((t_len, HK), F32), gate=((t_len, HK), F32), yc=((t_len, D_CONV), BF16))
    return pl.pallas_call(
        kern,
        grid=(n_tiles + 1,),
        in_specs=[
            pl.BlockSpec((1, t_len, D_MODEL), lambda s: tile_of(front(s))),
            pl.BlockSpec((1, t_len, D_MODEL), lambda s: tile_of(back(s))),
            pl.BlockSpec((1, SUBLANES, D_CONV), lambda s: seq_of(front(s), 2)),
            pl.BlockSpec((1, H_HGRN, DK, DV), lambda s: seq_of(back(s), 3)),
            _const_spec(lbl.shape), _const_spec(win.shape), _const_spec(cw.shape),
            _const_spec(cng.shape), _const_spec(hng.shape), _const_spec(gm.shape),
            _const_spec(wo.shape), _const_spec(g.shape), _const_spec(b.shape),
        ],
        out_specs=[
            pl.BlockSpec((1, t_len, D_MODEL), lambda s: tile_of(back(s))),
            pl.BlockSpec((1, SUBLANES, D_CONV), lambda s: seq_of(front(s), 2)),
            pl.BlockSpec((1, H_HGRN, DK, DV), lambda s: seq_of(back(s), 3)),
        ],
        out_shape=[
            jax.ShapeDtypeStruct(x.shape, F32),
            jax.ShapeDtypeStruct((bsz, SUBLANES, D_CONV), F32),
            jax.ShapeDtypeStruct((bsz, H_HGRN, DK, DV), F32),
        ],
        scratch_shapes=[pltpu.VMEM((1, H_HGRN, DV, DK), F32), pltpu.VMEM((1, SUBLANES, D_CONV), F32)]
        + [pltpu.VMEM(*buf_shapes[name]) for _ in range(2) for name in _BUF_FIELDS],
        compiler_params=pltpu.CompilerParams(
            dimension_semantics=("arbitrary",), vmem_limit_bytes=VMEM_LIMIT),
    )(x, x, conv_state, hgrn_state, lbl, win, cw, cng, hng, gm, wo, g, b)


def kernel(x_prompt, x_sample, state_conv, state_hgrn, lb_logits, ln1_g, ln1_b, ffn1_w_up, ffn1_w_down,
           ln2_g, ln2_b, w_in, conv_w, conv_norm_g, hgrn_norm_g, w_o, ln3_g, ln3_b, ffn2_w_up, ffn2_w_down):
    bsz, seq_len, _ = x_prompt.shape
    dec_b, dec_l, _ = x_sample.shape
    l = 0
    row = lambda a: a[l].reshape(1, -1)
    wup1, wdn1 = ffn1_w_up[l].astype(BF16), ffn1_w_down[l].astype(BF16)
    wup2, wdn2 = ffn2_w_up[l].astype(BF16), ffn2_w_down[l].astype(BF16)
    win, wo = w_in[l].astype(BF16), w_o[l].astype(BF16)
    lane_group = jnp.arange(D_CONV) // (D_CONV // N_CONV_GROUPS)
    gm = ((lane_group[:, None] == lane_group[None, :]) * (N_CONV_GROUPS / D_CONV)).astype(BF16)
    mix_w = (lb_logits, win, conv_w[l], row(conv_norm_g), row(hgrn_norm_g), gm, wo, row(ln2_g), row(ln2_b))

    def pad_conv(cs):
        return jnp.pad(cs, ((0, 0), (SUBLANES - (CONV_W - 1), 0), (0, 0)))

    xp = _ffn_ln(x_prompt.reshape(bsz * seq_len, D_MODEL), wup1, wdn1, row(ln1_g), row(ln1_b))
    zero_conv = jnp.zeros((bsz, SUBLANES, D_CONV), F32)
    zero_hgrn = jnp.zeros((bsz, H_HGRN, DK, DV), F32)
    xp, conv_p, hgrn_p = _mixer_ln_pipe(xp.reshape(bsz, seq_len, D_MODEL), zero_conv, zero_hgrn,
                                        PROMPT_CHUNK, *mix_w)
    yp = _ffn_ln(xp.reshape(bsz * seq_len, D_MODEL), wup2, wdn2, row(ln3_g), row(ln3_b))

    xs = _ffn_ln(x_sample.reshape(dec_b * dec_l, D_MODEL), wup1, wdn1, row(ln1_g), row(ln1_b))
    xs = jnp.pad(xs.reshape(dec_b, dec_l, D_MODEL), ((0, 0), (0, SUBLANES - dec_l), (0, 0)))
    xs, conv_s, hgrn_s = _mixer_ln_single(xs, pad_conv(state_conv[l]), state_hgrn[l], dec_l,
                                          SAMPLE_SEQS, *mix_w)
    ys = _ffn_ln(xs[:, :dec_l].reshape(dec_b * dec_l, D_MODEL), wup2, wdn2, row(ln3_g), row(ln3_b))

    return (yp.reshape(bsz, seq_len, D_MODEL), ys.reshape(dec_b, dec_l, D_MODEL),
            conv_p[None, :, 1 - CONV_W:], hgrn_p[None], conv_s[None, :, 1 - CONV_W:], hgrn_s[None])
```

```python
import functools
import math
import types

import jax
import jax.numpy as jnp
from jax import lax
from jax.experimental import pallas as pl
from jax.experimental.pallas import tpu as pltpu

D_MODEL = 1024
D_CONV = 512
N_CONV_GROUPS = 8
CONV_W = 3
H_HGRN = 4
DK = 128
DV = 128
HK = H_HGRN * DK
D_FF = 2816
ALPHA = 2.0 ** 0.25
EPS = 1e-5

SUBLANES = 8
LANES = 128
MXU_COLS = 256
VMEM_LIMIT = 56 * 1024 * 1024

SCORE_TILE = LANES
LEVEL_BLOCK = 64
FF_TILE = MXU_COLS
FFN_ROWS = 512
PROMPT_CHUNK = 256
PROMPT_SEQS = 2
SAMPLE_SEQS = 16

F32 = jnp.float32
BF16 = jnp.bfloat16


def _dot(a, b):
    return jnp.dot(a, b, preferred_element_type=F32)


def _dot_nt(a, b):
    return lax.dot_general(a, b, (((1,), (1,)), ((), ())), preferred_element_type=F32)


def _dot_tn(a, b):
    return lax.dot_general(a, b, (((0,), (0,)), ((), ())), preferred_element_type=F32)


def _sigmoid(x):
    return 0.5 * jnp.tanh(0.5 * x) + 0.5


def _layer_norm(z, g, b):
    mu = jnp.mean(z, axis=-1, keepdims=True)
    d = z - mu
    var = jnp.mean(d * d, axis=-1, keepdims=True)
    return d * lax.rsqrt(var + EPS) * g + b


def _const_spec(shape):
    nd = len(shape)
    return pl.BlockSpec(shape, lambda *_: (0,) * nd, pipeline_mode=pl.Buffered(1))


def _ffn_ln_kernel(x_ref, wup_ref, wdn_ref, g_ref, b_ref, o_ref):
    x = x_ref[...]
    xb = x.astype(BF16)
    acc = jnp.zeros(x.shape, F32)
    for j in range(D_FF // FF_TILE):
        c = j * FF_TILE
        a = _dot(xb, wup_ref[:, c:c + FF_TILE])
        b = _dot(xb, wup_ref[:, D_FF + c:D_FF + c + FF_TILE])
        h = (a * _sigmoid(a) * b).astype(BF16)
        acc = acc + _dot(h, wdn_ref[c:c + FF_TILE, :])
    o_ref[...] = _layer_norm(ALPHA * x + 0.5 * acc, g_ref[...], b_ref[...])


def _ffn_ln(x, wup, wdn, g, b):
    m = x.shape[0]
    tm = min(FFN_ROWS, m)
    assert m % tm == 0
    return pl.pallas_call(
        _ffn_ln_kernel,
        grid=(m // tm,),
        in_specs=[
            pl.BlockSpec((tm, D_MODEL), lambda i: (i, 0)),
            _const_spec(wup.shape),
            _const_spec(wdn.shape),
            _const_spec(g.shape),
            _const_spec(b.shape),
        ],
        out_specs=pl.BlockSpec((tm, D_MODEL), lambda i: (i, 0)),
        out_shape=jax.ShapeDtypeStruct((m, D_MODEL), F32),
        compiler_params=pltpu.CompilerParams(
            dimension_semantics=("parallel",), vmem_limit_bytes=VMEM_LIMIT),
    )(x, wup, wdn, g, b)


def _split_halves(a, h):
    rows, lanes = a.shape
    return jnp.split(a.reshape(rows // (2 * h), 2, h, lanes), 2, axis=1)


def _join_halves(lo, hi):
    n, _, h, lanes = lo.shape
    return jnp.concatenate([lo, hi], axis=1).reshape(n * 2 * h, lanes)


def _decay_step(q_pre, k_post, blk, h):
    rows, lanes = blk.shape
    if h < SUBLANES:
        groups = lambda a: a.reshape(rows // SUBLANES, SUBLANES, lanes)
        later = ((lax.broadcasted_iota(jnp.int32, (1, SUBLANES, 1), 1) // h) % 2) == 1
        other = pltpu.roll(groups(blk), h, 1)
        if 2 * h < SUBLANES:
            other = jnp.where(later, other, pltpu.roll(groups(blk), SUBLANES - h, 1))
        q_pre = (groups(q_pre) * jnp.where(later, other, 1.0)).reshape(rows, lanes)
        k_post = (groups(k_post) * jnp.where(later, 1.0, other)).reshape(rows, lanes)
        blk = (groups(blk) * other).reshape(rows, lanes)
    else:
        (q_lo, q_hi), (k_lo, k_hi), (b_lo, b_hi) = (_split_halves(a, h) for a in (q_pre, k_post, blk))
        q_pre = _join_halves(q_lo, q_hi * b_lo)
        k_post = _join_halves(k_lo * b_hi, k_hi)
        blk = _join_halves(b_lo * b_hi, b_lo * b_hi)
    return q_pre, k_post, blk


def _mixer_front(x, prev, lbl_ref, win_ref, cw_ref, cng_ref, gm_ref, *, n_seq, t_len, valid):
    rows = n_seq * t_len
    log2_c = int(math.log2(t_len))
    assert 1 << log2_c == t_len and t_len % SUBLANES == 0 and valid >= CONV_W - 1
    assert rows % LEVEL_BLOCK == 0 and (t_len <= LEVEL_BLOCK or t_len % LEVEL_BLOCK == 0)
    xb = x.astype(BF16)

    def proj(k):
        return _dot(xb, win_ref[:, k * D_CONV:(k + 1) * D_CONV])

    u = (proj(1) * proj(0)).reshape(n_seq, t_len, D_CONV)
    first = lax.broadcasted_iota(jnp.int32, (1, SUBLANES, 1), 1) == 0

    def delay(a, before):
        r = pltpu.roll(a, 1, 1)
        head = jnp.where(first, before, r[:, :SUBLANES])
        return head if t_len == SUBLANES else jnp.concatenate([head, r[:, SUBLANES:]], axis=1)

    cw = cw_ref[...]
    w0, w1, w2 = cw[0:1], cw[1:2], cw[2:3]
    pair = w0 * delay(u, prev[:, SUBLANES - 1:]) + w1 * u
    pair_before = w0 * prev[:, SUBLANES - 2:SUBLANES - 1] + w1 * prev[:, SUBLANES - 1:]
    conv = (delay(pair, pair_before) + w2 * u).reshape(rows, D_CONV)
    pad8 = t_len - valid
    assert pad8 < SUBLANES
    last8 = u[:, t_len - SUBLANES:]
    next_prev = last8 if pad8 == 0 else pltpu.roll(last8, pad8, 1)
    z = proj(2) * conv
    zz = z * z
    zz_hi = zz.astype(BF16)
    zz_lo = (zz - zz_hi.astype(F32)).astype(BF16)
    gm = gm_ref[...]
    ms = _dot(zz_hi, gm) + _dot(zz_lo, gm)
    y_conv = z * lax.rsqrt(ms + EPS) * cng_ref[...]

    lbl = lbl_ref[...]
    lmax = jnp.max(lbl, axis=0, keepdims=True)
    le = jnp.exp(lbl - lmax)
    lb = le[0:1] / jnp.sum(le, axis=0, keepdims=True)
    q = proj(3)
    qf = q * _sigmoid(q)
    f = lb + (1.0 - lb) * _sigmoid(proj(4))
    kk = 1.0 - f
    v = proj(5)
    gate_in = proj(6)
    gate = gate_in * _sigmoid(gate_in)
    if valid < t_len:
        live = lax.broadcasted_iota(jnp.int32, (rows, 1), 0) % t_len < valid
        f = jnp.where(live, f, 1.0)
        kk = jnp.where(live, kk, 0.0)

    return next_prev, types.SimpleNamespace(qf=qf, kk=kk, f=f, v=v, gate=gate, yc=y_conv.astype(BF16))


def _score_tile(qf, kk, f, t_len, level_rows):
    n_group = SCORE_TILE // SUBLANES
    top = min(t_len, SCORE_TILE)
    sc = [None] * n_group

    def place(level, p, takes):
        for n, i in enumerate(takes):
            hit = level_rows[i] == level
            part = p[n * SUBLANES:(n + 1) * SUBLANES]
            sc[i] = jnp.where(hit, part, 0.0 if sc[i] is None else sc[i])

    place(-1, _dot_nt(qf.astype(BF16), kk.astype(BF16)), range(n_group))
    blocks = [(qf[r:r + LEVEL_BLOCK] * f[r:r + LEVEL_BLOCK], kk[r:r + LEVEL_BLOCK], f[r:r + LEVEL_BLOCK])
              for r in range(0, SCORE_TILE, LEVEL_BLOCK)]
    for lh in range(int(math.log2(top))):
        h = 1 << lh
        if 2 * h > blocks[0][0].shape[0]:
            blocks = [tuple(jnp.concatenate([b[i] for b in blocks], axis=0) for i in range(3))]
        k_all = jnp.concatenate([b[1] for b in blocks], axis=0).astype(BF16)
        if h < SUBLANES:
            q_all = jnp.concatenate([b[0] for b in blocks], axis=0).astype(BF16)
            place(lh, _dot_nt(q_all, k_all), range(n_group))
        else:
            q_later = jnp.concatenate([_split_halves(b[0], h)[1].reshape(-1, DK) for b in blocks], axis=0)
            takes = [i for i in range(n_group) if (i * SUBLANES // h) % 2 == 1]
            place(lh, _dot_nt(q_later.astype(BF16), k_all), takes)
        blocks = [_decay_step(*b, h) for b in blocks]
    q_pre, k_post, blk = (jnp.concatenate([b[i] for b in blocks], axis=0) for i in range(3))
    return jnp.concatenate(sc, axis=0).astype(BF16), q_pre, k_post, blk


def _mixer_back(ops, x, st_ref, hng_ref, wo_ref, g_ref, b_ref, *, n_seq, t_len):
    rows = n_seq * t_len
    assert rows % SCORE_TILE == 0 and (t_len <= SCORE_TILE or t_len == 2 * SCORE_TILE)

    ti = lax.broadcasted_iota(jnp.int32, (SCORE_TILE, SCORE_TILE), 0)
    si = lax.broadcasted_iota(jnp.int32, (SCORE_TILE, SCORE_TILE), 1)
    txs = ti ^ si
    pair_level = jnp.where(ti > si, 0, jnp.where(ti == si, -1, -2))
    for lh in range(1, int(math.log2(SCORE_TILE))):
        pair_level = pair_level + jnp.where((ti > si) & (txs >= (1 << lh)), 1, 0)
    level_rows = [pair_level[i * SUBLANES:(i + 1) * SUBLANES] for i in range(SCORE_TILE // SUBLANES)]

    hng = hng_ref[...]
    y_heads = []
    for hd in range(H_HGRN):
        sl = slice(hd * DK, (hd + 1) * DK)
        vb = ops.v[:, sl].astype(BF16)
        tiles = [_score_tile(ops.qf[r:r + SCORE_TILE, sl], ops.kk[r:r + SCORE_TILE, sl],
                             ops.f[r:r + SCORE_TILE, sl], t_len, level_rows)
                 for r in range(0, rows, SCORE_TILE)]
        o_parts, qc, kc, decay = [], [], [], []
        if t_len <= SCORE_TILE:
            for it, (sc, q_pre, k_post, blk) in enumerate(tiles):
                o_parts.append(_dot(sc, vb[it * SCORE_TILE:(it + 1) * SCORE_TILE]))
                qc.append(q_pre)
                kc.append(k_post)
                decay += [blk[r:r + 1] for r in range(0, SCORE_TILE, t_len)]
        else:
            for g in range(n_seq):
                (sc0, q0, k0, b0), (sc1, q1, k1, b1) = tiles[2 * g], tiles[2 * g + 1]
                cross = _dot_nt(q1.astype(BF16), k0.astype(BF16)).astype(BF16)
                o_parts.append(_dot(sc0, vb[g * t_len:g * t_len + SCORE_TILE]))
                o_parts.append(_dot(jnp.concatenate([cross, sc1], axis=1), vb[g * t_len:(g + 1) * t_len]))
                qc += [q0, q1 * b0[0:1]]
                kc += [k0 * b1[0:1], k1]
                decay.append(b0[0:1] * b1[0:1])
        cat = lambda parts: parts[0] if len(parts) == 1 else jnp.concatenate(parts, axis=0)
        o, qc, kc = cat(o_parts), cat(qc).astype(BF16), cat(kc).astype(BF16)
        inter = []
        for g in range(n_seq):
            rs = slice(g * t_len, (g + 1) * t_len)
            st = st_ref[g, hd]
            inter.append(_dot_nt(qc[rs], st.astype(BF16)))
            st_ref[g, hd] = st * decay[g] + _dot_tn(vb[rs], kc[rs])
        o = o + cat(inter)
        ms_h = jnp.mean(o * o, axis=-1, keepdims=True)
        y_heads.append((o * lax.rsqrt(ms_h + EPS) * hng[:, sl] * ops.gate[:, sl]).astype(BF16))

    y = jnp.concatenate([ops.yc] + y_heads, axis=-1)
    mixed = _dot(y, wo_ref[...])
    return _layer_norm(ALPHA * x + mixed, g_ref[...], b_ref[...])


def _mixer_kernel(x_ref, cs_ref, hs_ref, lbl_ref, win_ref, cw_ref, cng_ref, hng_ref, gm_ref,
                  wo_ref, g_ref, b_ref, o_ref, cso_ref, hso_ref, st_ref, carry_ref,
                  *, n_seq, t_len, valid, n_tiles):
    j = pl.program_id(1)
    assert valid == t_len or n_tiles == 1

    @pl.when(j == 0)
    def _():
        carry_ref[...] = cs_ref[...]
        for g in range(n_seq):
            for hd in range(H_HGRN):
                st_ref[g, hd] = hs_ref[g, hd].T

    x = x_ref[...].reshape(n_seq * t_len, D_MODEL)
    next_prev, ops = _mixer_front(x, carry_ref[...], lbl_ref, win_ref, cw_ref, cng_ref, gm_ref,
                                  n_seq=n_seq, t_len=t_len, valid=valid)
    out = _mixer_back(ops, x, st_ref, hng_ref, wo_ref, g_ref, b_ref, n_seq=n_seq, t_len=t_len)
    o_ref[...] = out.reshape(n_seq, t_len, D_MODEL)
    carry_ref[...] = next_prev

    @pl.when(j == n_tiles - 1)
    def _():
        cso_ref[...] = next_prev
        for g in range(n_seq):
            for hd in range(H_HGRN):
                hso_ref[g, hd] = st_ref[g, hd].T


def _mixer_ln(x, conv_state, hgrn_state, valid, n_seq, t_len, lbl, win, cw, cng, hng, gm, wo, g, b):
    bsz, seq_len, _ = x.shape
    assert bsz % n_seq == 0 and seq_len % t_len == 0
    n_tiles = seq_len // t_len
    kern = functools.partial(_mixer_kernel, n_seq=n_seq, t_len=t_len,
                             valid=valid - (seq_len - t_len), n_tiles=n_tiles)
    seq_block = lambda *tail: pl.BlockSpec((n_seq,) + tail, lambda i, j: (i,) + (0,) * len(tail))
    tile_block = pl.BlockSpec((n_seq, t_len, D_MODEL), lambda i, j: (i, j, 0))
    return pl.pallas_call(
        kern,
        grid=(bsz // n_seq, n_tiles),
        in_specs=[
            tile_block, seq_block(SUBLANES, D_CONV), seq_block(H_HGRN, DK, DV),
            _const_spec(lbl.shape), _const_spec(win.shape), _const_spec(cw.shape),
            _const_spec(cng.shape), _const_spec(hng.shape), _const_spec(gm.shape),
            _const_spec(wo.shape), _const_spec(g.shape), _const_spec(b.shape),
        ],
        out_specs=[tile_block, seq_block(SUBLANES, D_CONV), seq_block(H_HGRN, DK, DV)],
        out_shape=[
            jax.ShapeDtypeStruct(x.shape, F32),
            jax.ShapeDtypeStruct((bsz, SUBLANES, D_CONV), F32),
            jax.ShapeDtypeStruct((bsz, H_HGRN, DK, DV), F32),
        ],
        scratch_shapes=[pltpu.VMEM((n_seq, H_HGRN, DV, DK), F32), pltpu.VMEM((n_seq, SUBLANES, D_CONV), F32)],
        compiler_params=pltpu.CompilerParams(
            dimension_semantics=("parallel", "arbitrary"), vmem_limit_bytes=VMEM_LIMIT),
    )(x, conv_state, hgrn_state, lbl, win, cw, cng, hng, gm, wo, g, b)


def kernel(x_prompt, x_sample, state_conv, state_hgrn, lb_logits, ln1_g, ln1_b, ffn1_w_up, ffn1_w_down,
           ln2_g, ln2_b, w_in, conv_w, conv_norm_g, hgrn_norm_g, w_o, ln3_g, ln3_b, ffn2_w_up, ffn2_w_down):
    bsz, seq_len, _ = x_prompt.shape
    dec_b, dec_l, _ = x_sample.shape
    l = 0
    row = lambda a: a[l].reshape(1, -1)
    wup1, wdn1 = ffn1_w_up[l].astype(BF16), ffn1_w_down[l].astype(BF16)
    wup2, wdn2 = ffn2_w_up[l].astype(BF16), ffn2_w_down[l].astype(BF16)
    win, wo = w_in[l].astype(BF16), w_o[l].astype(BF16)
    lane_group = jnp.arange(D_CONV) // (D_CONV // N_CONV_GROUPS)
    gm = ((lane_group[:, None] == lane_group[None, :]) * (N_CONV_GROUPS / D_CONV)).astype(BF16)
    mix_w = (lb_logits, win, conv_w[l], row(conv_norm_g), row(hgrn_norm_g), gm, wo, row(ln2_g), row(ln2_b))

    def pad_conv(cs):
        return jnp.pad(cs, ((0, 0), (SUBLANES - (CONV_W - 1), 0), (0, 0)))

    xp = _ffn_ln(x_prompt.reshape(bsz * seq_len, D_MODEL), wup1, wdn1, row(ln1_g), row(ln1_b))
    zero_conv = jnp.zeros((bsz, SUBLANES, D_CONV), F32)
    zero_hgrn = jnp.zeros((bsz, H_HGRN, DK, DV), F32)
    xp, conv_p, hgrn_p = _mixer_ln(xp.reshape(bsz, seq_len, D_MODEL), zero_conv, zero_hgrn, seq_len,
                                   PROMPT_SEQS, PROMPT_CHUNK, *mix_w)
    yp = _ffn_ln(xp.reshape(bsz * seq_len, D_MODEL), wup2, wdn2, row(ln3_g), row(ln3_b))

    xs = _ffn_ln(x_sample.reshape(dec_b * dec_l, D_MODEL), wup1, wdn1, row(ln1_g), row(ln1_b))
    xs = jnp.pad(xs.reshape(dec_b, dec_l, D_MODEL), ((0, 0), (0, SUBLANES - dec_l), (0, 0)))
    xs, conv_s, hgrn_s = _mixer_ln(xs, pad_conv(state_conv[l]), state_hgrn[l], dec_l,
                                   SAMPLE_SEQS, SUBLANES, *mix_w)
    ys = _ffn_ln(xs[:, :dec_l].reshape(dec_b * dec_l, D_MODEL), wup2, wdn2, row(ln3_g), row(ln3_b))

    return (yp.reshape(bsz, seq_len, D_MODEL), ys.reshape(dec_b, dec_l, D_MODEL),
            conv_p[None, :, 1 - CONV_W:], hgrn_p[None], conv_s[None, :, 1 - CONV_W:], hgrn_s[None])
```

```python
import functools
import math
import types

import jax
import jax.numpy as jnp
from jax import lax
from jax.experimental import pallas as pl
from jax.experimental.pallas import tpu as pltpu

D_MODEL = 1024
D_CONV = 512
N_CONV_GROUPS = 8
CONV_W = 3
H_HGRN = 4
DK = 128
DV = 128
HK = H_HGRN * DK
D_FF = 2816
ALPHA = 2.0 ** 0.25
EPS = 1e-5

SUBLANES = 8
LANES = 128
MXU_COLS = 256
VMEM_LIMIT = 56 * 1024 * 1024

SCORE_TILE = LANES
LEVEL_BLOCK = 64
FF_TILE = MXU_COLS
FFN_ROWS = 512
LN_CHUNKS = 8
PROMPT_CHUNK = 256
PROMPT_SEQS = 2
SAMPLE_SEQS = 16

F32 = jnp.float32
BF16 = jnp.bfloat16


def _dot(a, b):
    return jnp.dot(a, b, preferred_element_type=F32)


def _dot_nt(a, b):
    return lax.dot_general(a, b, (((1,), (1,)), ((), ())), preferred_element_type=F32)


def _dot_tn(a, b):
    return lax.dot_general(a, b, (((0,), (0,)), ((), ())), preferred_element_type=F32)


def _sigmoid(x):
    return 0.5 * jnp.tanh(0.5 * x) + 0.5


def _layer_norm(z, g, b):
    mu = jnp.mean(z, axis=-1, keepdims=True)
    d = z - mu
    var = jnp.mean(d * d, axis=-1, keepdims=True)
    return d * lax.rsqrt(var + EPS) * g + b


def _always_zero(y):
    rows, cols = y.shape
    bits = pltpu.bitcast(y, jnp.uint32).reshape(rows // SUBLANES, SUBLANES, cols)
    merged = functools.reduce(jnp.bitwise_or, [bits[i] for i in range(rows // SUBLANES)])
    merged = functools.reduce(jnp.bitwise_or, [merged[:, i:i + LANES] for i in range(0, cols, LANES)])
    return ((merged >> 16) >> 16).astype(F32)


def _const_spec(shape):
    nd = len(shape)
    return pl.BlockSpec(shape, lambda *_: (0,) * nd, pipeline_mode=pl.Buffered(1))


def _ffn_ln_kernel(x_ref, xprev_ref, wup_ref, wdn_ref, g_ref, b_ref, o_ref, acc0_ref, acc1_ref, *, n_tiles):
    s = pl.program_id(0)
    acc_refs = (acc0_ref, acc1_ref)
    tm = x_ref.shape[0]
    chunk = tm // LN_CHUNKS

    def finish(acc_ref, c):
        rows = slice(c * chunk, (c + 1) * chunk)
        y = _layer_norm(ALPHA * xprev_ref[rows, :] + 0.5 * acc_ref[rows, :], g_ref[...], b_ref[...])
        o_ref[rows, :] = y
        return y

    def step(read_ref, write_ref):
        xb = x_ref[...].astype(BF16)
        acc = jnp.zeros(x_ref.shape, F32)
        for j in range(D_FF // FF_TILE):
            c = j * FF_TILE
            a = _dot(xb, wup_ref[:, c:c + FF_TILE])
            b = _dot(xb, wup_ref[:, D_FF + c:D_FF + c + FF_TILE])
            h = a * _sigmoid(a) * b
            if j < LN_CHUNKS:
                zero = _always_zero(finish(read_ref, j))
                h = jnp.concatenate([h[:SUBLANES] + jnp.concatenate([zero] * (FF_TILE // LANES), axis=1),
                                     h[SUBLANES:]], axis=0)
            acc = acc + _dot(h.astype(BF16), wdn_ref[c:c + FF_TILE, :])
        write_ref[...] = acc

    @pl.when(s == 0)
    def _():
        acc1_ref[...] = jnp.zeros(acc1_ref.shape, F32)

    for parity in range(2):
        pl.when((s < n_tiles) & (s % 2 == parity))(
            functools.partial(step, acc_refs[1 - parity], acc_refs[parity]))
    @pl.when(s == n_tiles)
    def _():
        for c in range(LN_CHUNKS):
            finish(acc_refs[(n_tiles - 1) % 2], c)


def _ffn_ln(x, wup, wdn, g, b):
    m = x.shape[0]
    tm = min(FFN_ROWS, m)
    assert m % tm == 0
    n_tiles = m // tm
    return pl.pallas_call(
        functools.partial(_ffn_ln_kernel, n_tiles=n_tiles),
        grid=(n_tiles + 1,),
        in_specs=[
            pl.BlockSpec((tm, D_MODEL), lambda s: (jnp.minimum(s, n_tiles - 1), 0)),
            pl.BlockSpec((tm, D_MODEL), lambda s: (jnp.maximum(s - 1, 0), 0)),
            _const_spec(wup.shape),
            _const_spec(wdn.shape),
            _const_spec(g.shape),
            _const_spec(b.shape),
        ],
        out_specs=pl.BlockSpec((tm, D_MODEL), lambda s: (jnp.maximum(s - 1, 0), 0)),
        out_shape=jax.ShapeDtypeStruct((m, D_MODEL), F32),
        scratch_shapes=[pltpu.VMEM((tm, D_MODEL), F32), pltpu.VMEM((tm, D_MODEL), F32)],
        compiler_params=pltpu.CompilerParams(
            dimension_semantics=("arbitrary",), vmem_limit_bytes=VMEM_LIMIT),
    )(x, x, wup, wdn, g, b)


def _split_halves(a, h):
    rows, lanes = a.shape
    return jnp.split(a.reshape(rows // (2 * h), 2, h, lanes), 2, axis=1)


def _join_halves(lo, hi):
    n, _, h, lanes = lo.shape
    return jnp.concatenate([lo, hi], axis=1).reshape(n * 2 * h, lanes)


def _decay_step(q_pre, k_post, blk, h):
    rows, lanes = blk.shape
    if h < SUBLANES:
        groups = lambda a: a.reshape(rows // SUBLANES, SUBLANES, lanes)
        later = ((lax.broadcasted_iota(jnp.int32, (1, SUBLANES, 1), 1) // h) % 2) == 1
        other = pltpu.roll(groups(blk), h, 1)
        if 2 * h < SUBLANES:
            other = jnp.where(later, other, pltpu.roll(groups(blk), SUBLANES - h, 1))
        q_pre = (groups(q_pre) * jnp.where(later, other, 1.0)).reshape(rows, lanes)
        k_post = (groups(k_post) * jnp.where(later, 1.0, other)).reshape(rows, lanes)
        blk = (groups(blk) * other).reshape(rows, lanes)
    else:
        (q_lo, q_hi), (k_lo, k_hi), (b_lo, b_hi) = (_split_halves(a, h) for a in (q_pre, k_post, blk))
        q_pre = _join_halves(q_lo, q_hi * b_lo)
        k_post = _join_halves(k_lo * b_hi, k_hi)
        blk = _join_halves(b_lo * b_hi, b_lo * b_hi)
    return q_pre, k_post, blk


def _mixer_front(x, prev, lbl_ref, win_ref, cw_ref, cng_ref, gm_ref, *, n_seq, t_len, valid):
    rows = n_seq * t_len
    log2_c = int(math.log2(t_len))
    assert 1 << log2_c == t_len and t_len % SUBLANES == 0 and valid >= CONV_W - 1
    assert rows % LEVEL_BLOCK == 0 and (t_len <= LEVEL_BLOCK or t_len % LEVEL_BLOCK == 0)
    xb = x.astype(BF16)

    def proj(k):
        return _dot(xb, win_ref[:, k * D_CONV:(k + 1) * D_CONV])

    u = (proj(1) * proj(0)).reshape(n_seq, t_len, D_CONV)
    first = lax.broadcasted_iota(jnp.int32, (1, SUBLANES, 1), 1) == 0

    def delay(a, before):
        r = pltpu.roll(a, 1, 1)
        head = jnp.where(first, before, r[:, :SUBLANES])
        return head if t_len == SUBLANES else jnp.concatenate([head, r[:, SUBLANES:]], axis=1)

    cw = cw_ref[...]
    w0, w1, w2 = cw[0:1], cw[1:2], cw[2:3]
    pair = w0 * delay(u, prev[:, SUBLANES - 1:]) + w1 * u
    pair_before = w0 * prev[:, SUBLANES - 2:SUBLANES - 1] + w1 * prev[:, SUBLANES - 1:]
    conv = (delay(pair, pair_before) + w2 * u).reshape(rows, D_CONV)
    pad8 = t_len - valid
    assert pad8 < SUBLANES
    last8 = u[:, t_len - SUBLANES:]
    next_prev = last8 if pad8 == 0 else pltpu.roll(last8, pad8, 1)
    z = proj(2) * conv
    zz = z * z
    zz_hi = zz.astype(BF16)
    zz_lo = (zz - zz_hi.astype(F32)).astype(BF16)
    gm = gm_ref[...]
    ms = _dot(zz_hi, gm) + _dot(zz_lo, gm)
    y_conv = z * lax.rsqrt(ms + EPS) * cng_ref[...]

    lbl = lbl_ref[...]
    lmax = jnp.max(lbl, axis=0, keepdims=True)
    le = jnp.exp(lbl - lmax)
    lb = le[0:1] / jnp.sum(le, axis=0, keepdims=True)
    q = proj(3)
    qf = q * _sigmoid(q)
    f = lb + (1.0 - lb) * _sigmoid(proj(4))
    kk = 1.0 - f
    v = proj(5)
    gate_in = proj(6)
    gate = gate_in * _sigmoid(gate_in)
    if valid < t_len:
        live = lax.broadcasted_iota(jnp.int32, (rows, 1), 0) % t_len < valid
        f = jnp.where(live, f, 1.0)
        kk = jnp.where(live, kk, 0.0)

    return next_prev, types.SimpleNamespace(qf=qf, kk=kk, f=f, v=v, gate=gate, yc=y_conv.astype(BF16))


def _score_tile(qf, kk, f, t_len, level_rows):
    n_group = SCORE_TILE // SUBLANES
    top = min(t_len, SCORE_TILE)
    sc = [None] * n_group

    def place(level, p, takes):
        for n, i in enumerate(takes):
            hit = level_rows[i] == level
            part = p[n * SUBLANES:(n + 1) * SUBLANES]
            sc[i] = jnp.where(hit, part, 0.0 if sc[i] is None else sc[i])

    place(-1, _dot_nt(qf.astype(BF16), kk.astype(BF16)), range(n_group))
    blocks = [(qf[r:r + LEVEL_BLOCK] * f[r:r + LEVEL_BLOCK], kk[r:r + LEVEL_BLOCK], f[r:r + LEVEL_BLOCK])
              for r in range(0, SCORE_TILE, LEVEL_BLOCK)]
    for lh in range(int(math.log2(top))):
        h = 1 << lh
        if 2 * h > blocks[0][0].shape[0]:
            blocks = [tuple(jnp.concatenate([b[i] for b in blocks], axis=0) for i in range(3))]
        k_all = jnp.concatenate([b[1] for b in blocks], axis=0).astype(BF16)
        if h < SUBLANES:
            q_all = jnp.concatenate([b[0] for b in blocks], axis=0).astype(BF16)
            place(lh, _dot_nt(q_all, k_all), range(n_group))
        else:
            q_later = jnp.concatenate([_split_halves(b[0], h)[1].reshape(-1, DK) for b in blocks], axis=0)
            takes = [i for i in range(n_group) if (i * SUBLANES // h) % 2 == 1]
            place(lh, _dot_nt(q_later.astype(BF16), k_all), takes)
        blocks = [_decay_step(*b, h) for b in blocks]
    q_pre, k_post, blk = (jnp.concatenate([b[i] for b in blocks], axis=0) for i in range(3))
    return jnp.concatenate(sc, axis=0).astype(BF16), q_pre, k_post, blk


def _mixer_back(ops, x, st_ref, hng_ref, wo_ref, g_ref, b_ref, *, n_seq, t_len):
    rows = n_seq * t_len
    assert rows % SCORE_TILE == 0 and (t_len <= SCORE_TILE or t_len == 2 * SCORE_TILE)

    ti = lax.broadcasted_iota(jnp.int32, (SCORE_TILE, SCORE_TILE), 0)
    si = lax.broadcasted_iota(jnp.int32, (SCORE_TILE, SCORE_TILE), 1)
    txs = ti ^ si
    pair_level = jnp.where(ti > si, 0, jnp.where(ti == si, -1, -2))
    for lh in range(1, int(math.log2(SCORE_TILE))):
        pair_level = pair_level + jnp.where((ti > si) & (txs >= (1 << lh)), 1, 0)
    level_rows = [pair_level[i * SUBLANES:(i + 1) * SUBLANES] for i in range(SCORE_TILE // SUBLANES)]

    hng = hng_ref[...]
    y_heads = []
    for hd in range(H_HGRN):
        sl = slice(hd * DK, (hd + 1) * DK)
        vb = ops.v[:, sl].astype(BF16)
        tiles = [_score_tile(ops.qf[r:r + SCORE_TILE, sl], ops.kk[r:r + SCORE_TILE, sl],
                             ops.f[r:r + SCORE_TILE, sl], t_len, level_rows)
                 for r in range(0, rows, SCORE_TILE)]
        o_parts, qc, kc, decay = [], [], [], []
        if t_len <= SCORE_TILE:
            for it, (sc, q_pre, k_post, blk) in enumerate(tiles):
                o_parts.append(_dot(sc, vb[it * SCORE_TILE:(it + 1) * SCORE_TILE]))
                qc.append(q_pre)
                kc.append(k_post)
                decay += [blk[r:r + 1] for r in range(0, SCORE_TILE, t_len)]
        else:
            for g in range(n_seq):
                (sc0, q0, k0, b0), (sc1, q1, k1, b1) = tiles[2 * g], tiles[2 * g + 1]
                cross = _dot_nt(q1.astype(BF16), k0.astype(BF16)).astype(BF16)
                o_parts.append(_dot(sc0, vb[g * t_len:g * t_len + SCORE_TILE]))
                o_parts.append(_dot(jnp.concatenate([cross, sc1], axis=1), vb[g * t_len:(g + 1) * t_len]))
                qc += [q0, q1 * b0[0:1]]
                kc += [k0 * b1[0:1], k1]
                decay.append(b0[0:1] * b1[0:1])
        cat = lambda parts: parts[0] if len(parts) == 1 else jnp.concatenate(parts, axis=0)
        o, qc, kc = cat(o_parts), cat(qc).astype(BF16), cat(kc).astype(BF16)
        inter = []
        for g in range(n_seq):
            rs = slice(g * t_len, (g + 1) * t_len)
            st = st_ref[g, hd]
            inter.append(_dot_nt(qc[rs], st.astype(BF16)))
            st_ref[g, hd] = st * decay[g] + _dot_tn(vb[rs], kc[rs])
        o = o + cat(inter)
        ms_h = jnp.mean(o * o, axis=-1, keepdims=True)
        y_heads.append((o * lax.rsqrt(ms_h + EPS) * hng[:, sl] * ops.gate[:, sl]).astype(BF16))

    y = jnp.concatenate([ops.yc] + y_heads, axis=-1)
    mixed = _dot(y, wo_ref[...])
    return _layer_norm(ALPHA * x + mixed, g_ref[...], b_ref[...])


def _mixer_kernel(x_ref, cs_ref, hs_ref, lbl_ref, win_ref, cw_ref, cng_ref, hng_ref, gm_ref,
                  wo_ref, g_ref, b_ref, o_ref, cso_ref, hso_ref, st_ref, carry_ref,
                  *, n_seq, t_len, valid, n_tiles):
    j = pl.program_id(1)
    assert valid == t_len or n_tiles == 1

    @pl.when(j == 0)
    def _():
        carry_ref[...] = cs_ref[...]
        for g in range(n_seq):
            for hd in range(H_HGRN):
                st_ref[g, hd] = hs_ref[g, hd].T

    x = x_ref[...].reshape(n_seq * t_len, D_MODEL)
    next_prev, ops = _mixer_front(x, carry_ref[...], lbl_ref, win_ref, cw_ref, cng_ref, gm_ref,
                                  n_seq=n_seq, t_len=t_len, valid=valid)
    out = _mixer_back(ops, x, st_ref, hng_ref, wo_ref, g_ref, b_ref, n_seq=n_seq, t_len=t_len)
    o_ref[...] = out.reshape(n_seq, t_len, D_MODEL)
    carry_ref[...] = next_prev

    @pl.when(j == n_tiles - 1)
    def _():
        cso_ref[...] = next_prev
        for g in range(n_seq):
            for hd in range(H_HGRN):
                hso_ref[g, hd] = st_ref[g, hd].T


def _mixer_ln(x, conv_state, hgrn_state, valid, n_seq, t_len, lbl, win, cw, cng, hng, gm, wo, g, b):
    bsz, seq_len, _ = x.shape
    assert bsz % n_seq == 0 and seq_len % t_len == 0
    n_tiles = seq_len // t_len
    kern = functools.partial(_mixer_kernel, n_seq=n_seq, t_len=t_len,
                             valid=valid - (seq_len - t_len), n_tiles=n_tiles)
    seq_block = lambda *tail: pl.BlockSpec((n_seq,) + tail, lambda i, j: (i,) + (0,) * len(tail))
    tile_block = pl.BlockSpec((n_seq, t_len, D_MODEL), lambda i, j: (i, j, 0))
    return pl.pallas_call(
        kern,
        grid=(bsz // n_seq, n_tiles),
        in_specs=[
            tile_block, seq_block(SUBLANES, D_CONV), seq_block(H_HGRN, DK, DV),
            _const_spec(lbl.shape), _const_spec(win.shape), _const_spec(cw.shape),
            _const_spec(cng.shape), _const_spec(hng.shape), _const_spec(gm.shape),
            _const_spec(wo.shape), _const_spec(g.shape), _const_spec(b.shape),
        ],
        out_specs=[tile_block, seq_block(SUBLANES, D_CONV), seq_block(H_HGRN, DK, DV)],
        out_shape=[
            jax.ShapeDtypeStruct(x.shape, F32),
            jax.ShapeDtypeStruct((bsz, SUBLANES, D_CONV), F32),
            jax.ShapeDtypeStruct((bsz, H_HGRN, DK, DV), F32),
        ],
        scratch_shapes=[pltpu.VMEM((n_seq, H_HGRN, DV, DK), F32), pltpu.VMEM((n_seq, SUBLANES, D_CONV), F32)],
        compiler_params=pltpu.CompilerParams(
            dimension_semantics=("parallel", "arbitrary"), vmem_limit_bytes=VMEM_LIMIT),
    )(x, conv_state, hgrn_state, lbl, win, cw, cng, hng, gm, wo, g, b)


def kernel(x_prompt, x_sample, state_conv, state_hgrn, lb_logits, ln1_g, ln1_b, ffn1_w_up, ffn1_w_down,
           ln2_g, ln2_b, w_in, conv_w, conv_norm_g, hgrn_norm_g, w_o, ln3_g, ln3_b, ffn2_w_up, ffn2_w_down):
    bsz, seq_len, _ = x_prompt.shape
    dec_b, dec_l, _ = x_sample.shape
    l = 0
    row = lambda a: a[l].reshape(1, -1)
    wup1, wdn1 = ffn1_w_up[l].astype(BF16), ffn1_w_down[l].astype(BF16)
    wup2, wdn2 = ffn2_w_up[l].astype(BF16), ffn2_w_down[l].astype(BF16)
    win, wo = w_in[l].astype(BF16), w_o[l].astype(BF16)
    lane_group = jnp.arange(D_CONV) // (D_CONV // N_CONV_GROUPS)
    gm = ((lane_group[:, None] == lane_group[None, :]) * (N_CONV_GROUPS / D_CONV)).astype(BF16)
    mix_w = (lb_logits, win, conv_w[l], row(conv_norm_g), row(hgrn_norm_g), gm, wo, row(ln2_g), row(ln2_b))

    def pad_conv(cs):
        return jnp.pad(cs, ((0, 0), (SUBLANES - (CONV_W - 1), 0), (0, 0)))

    xp = _ffn_ln(x_prompt.reshape(bsz * seq_len, D_MODEL), wup1, wdn1, row(ln1_g), row(ln1_b))
    zero_conv = jnp.zeros((bsz, SUBLANES, D_CONV), F32)
    zero_hgrn = jnp.zeros((bsz, H_HGRN, DK, DV), F32)
    xp, conv_p, hgrn_p = _mixer_ln(xp.reshape(bsz, seq_len, D_MODEL), zero_conv, zero_hgrn, seq_len,
                                   PROMPT_SEQS, PROMPT_CHUNK, *mix_w)
    yp = _ffn_ln(xp.reshape(bsz * seq_len, D_MODEL), wup2, wdn2, row(ln3_g), row(ln3_b))

    xs = _ffn_ln(x_sample.reshape(dec_b * dec_l, D_MODEL), wup1, wdn1, row(ln1_g), row(ln1_b))
    xs = jnp.pad(xs.reshape(dec_b, dec_l, D_MODEL), ((0, 0), (0, SUBLANES - dec_l), (0, 0)))
    xs, conv_s, hgrn_s = _mixer_ln(xs, pad_conv(state_conv[l]), state_hgrn[l], dec_l,
                                   SAMPLE_SEQS, SUBLANES, *mix_w)
    ys = _ffn_ln(xs[:, :dec_l].reshape(dec_b * dec_l, D_MODEL), wup2, wdn2, row(ln3_g), row(ln3_b))

    return (yp.reshape(bsz, seq_len, D_MODEL), ys.reshape(dec_b, dec_l, D_MODEL),
            conv_p[None, :, 1 - CONV_W:], hgrn_p[None], conv_s[None, :, 1 - CONV_W:], hgrn_s[None])
```

```python
import functools
import math
import types

import jax
import jax.numpy as jnp
from jax import lax
from jax.experimental import pallas as pl
from jax.experimental.pallas import tpu as pltpu

D_MODEL = 1024
D_CONV = 512
N_CONV_GROUPS = 8
CONV_W = 3
H_HGRN = 4
DK = 128
DV = 128
HK = H_HGRN * DK
D_FF = 2816
ALPHA = 2.0 ** 0.25
EPS = 1e-5

SUBLANES = 8
LANES = 128
BF16_SUBLANES = 16
MXU_COLS = 256
VMEM_LIMIT = 56 * 1024 * 1024

SCORE_TILE = LANES
LEVEL_BLOCK = 64
FF_TILE = MXU_COLS
FFN_ROWS = 1024
LN_CHUNKS = 8
PROMPT_CHUNK = 256
PROMPT_SEQS = 2
SAMPLE_SEQS = 16

F32 = jnp.float32
BF16 = jnp.bfloat16


def _dot(a, b):
    return jnp.dot(a, b, preferred_element_type=F32)


def _dot_nt(a, b):
    return lax.dot_general(a, b, (((1,), (1,)), ((), ())), preferred_element_type=F32)


def _dot_tn(a, b):
    return lax.dot_general(a, b, (((0,), (0,)), ((), ())), preferred_element_type=F32)


def _sigmoid(x):
    return 0.5 * jnp.tanh(0.5 * x) + 0.5


def _layer_norm(z, g, b):
    mu = jnp.mean(z, axis=-1, keepdims=True)
    d = z - mu
    var = jnp.mean(d * d, axis=-1, keepdims=True)
    return d * lax.rsqrt(var + EPS) * g + b


def _always_zero(y):
    rows, cols = y.shape
    bits = pltpu.bitcast(y, jnp.uint32).reshape(rows // SUBLANES, SUBLANES, cols)
    merged = functools.reduce(jnp.bitwise_or, [bits[i] for i in range(rows // SUBLANES)])
    merged = functools.reduce(jnp.bitwise_or, [merged[:, i:i + LANES] for i in range(0, cols, LANES)])
    return ((merged >> 16) >> 16).astype(F32)


def _const_spec(shape):
    nd = len(shape)
    return pl.BlockSpec(shape, lambda *_: (0,) * nd, pipeline_mode=pl.Buffered(1))


def _cast_specs(arrays, n_steps, step_of):
    specs, shapes = [], []
    for a in arrays:
        rows, cols = a.shape
        assert rows % n_steps == 0 and (rows // n_steps) % BF16_SUBLANES == 0
        specs.append(pl.BlockSpec((rows // n_steps, cols), lambda *idx: (step_of(*idx), 0)))
        shapes.append(jax.ShapeDtypeStruct(a.shape, BF16))
    return specs, shapes


def _cast_chunks(src_refs, dst_refs):
    for src, dst in zip(src_refs, dst_refs):
        dst[...] = src[...].astype(BF16)


def _ffn_ln_kernel(*refs, n_tiles, n_cast):
    x_ref, wup_ref, wdn_ref, g_ref, b_ref = refs[:5]
    cast_in, (o_ref, *cast_out) = refs[5:5 + n_cast], refs[5 + n_cast:6 + 2 * n_cast]
    z_refs = refs[6 + 2 * n_cast:]
    s = pl.program_id(0)
    tm = x_ref.shape[0]
    chunk = tm // LN_CHUNKS
    _cast_chunks(cast_in, cast_out)

    def finish(z_ref, c):
        rows = slice(c * chunk, (c + 1) * chunk)
        y = _layer_norm(z_ref[rows, :], g_ref[...], b_ref[...])
        o_ref[rows, :] = y
        return y

    def step(read_ref, write_ref):
        x = x_ref[...]
        xb = x.astype(BF16)
        acc = jnp.zeros(x.shape, F32)
        for j in range(D_FF // FF_TILE):
            c = j * FF_TILE
            a = _dot(xb, wup_ref[:, c:c + FF_TILE])
            b = _dot(xb, wup_ref[:, D_FF + c:D_FF + c + FF_TILE])
            h = a * _sigmoid(a) * b
            if j < LN_CHUNKS:
                zero = _always_zero(finish(read_ref, j))
                h = jnp.concatenate([h[:SUBLANES] + jnp.concatenate([zero] * (FF_TILE // LANES), axis=1),
                                     h[SUBLANES:]], axis=0)
            acc = acc + _dot(h.astype(BF16), wdn_ref[c:c + FF_TILE, :])
        write_ref[...] = ALPHA * x + 0.5 * acc

    @pl.when(s == 0)
    def _():
        z_refs[1][...] = jnp.zeros(z_refs[1].shape, F32)

    for parity in range(2):
        pl.when((s < n_tiles) & (s % 2 == parity))(functools.partial(step, z_refs[1 - parity], z_refs[parity]))

    @pl.when(s == n_tiles)
    def _():
        for c in range(LN_CHUNKS):
            finish(z_refs[(n_tiles - 1) % 2], c)


def _ffn_ln(x, wup, wdn, g, b, cast=()):
    m = x.shape[0]
    tm = min(FFN_ROWS, m)
    assert m % tm == 0
    n_tiles = m // tm
    tile = lambda s: jnp.minimum(s, n_tiles - 1)
    cast_specs, cast_shapes = _cast_specs(cast, n_tiles, tile)
    y, *casted = pl.pallas_call(
        functools.partial(_ffn_ln_kernel, n_tiles=n_tiles, n_cast=len(cast)),
        grid=(n_tiles + 1,),
        in_specs=[
            pl.BlockSpec((tm, D_MODEL), lambda s: (tile(s), 0)),
            _const_spec(wup.shape),
            _const_spec(wdn.shape),
            _const_spec(g.shape),
            _const_spec(b.shape),
        ] + cast_specs,
        out_specs=[pl.BlockSpec((tm, D_MODEL), lambda s: (jnp.maximum(s - 1, 0), 0))] + cast_specs,
        out_shape=[jax.ShapeDtypeStruct((m, D_MODEL), F32)] + cast_shapes,
        scratch_shapes=[pltpu.VMEM((tm, D_MODEL), F32), pltpu.VMEM((tm, D_MODEL), F32)],
        compiler_params=pltpu.CompilerParams(
            dimension_semantics=("arbitrary",), vmem_limit_bytes=VMEM_LIMIT),
    )(x, wup, wdn, g, b, *cast)
    return y, casted


def _split_halves(a, h):
    rows, lanes = a.shape
    return jnp.split(a.reshape(rows // (2 * h), 2, h, lanes), 2, axis=1)


def _join_halves(lo, hi):
    n, _, h, lanes = lo.shape
    return jnp.concatenate([lo, hi], axis=1).reshape(n * 2 * h, lanes)


def _decay_step(q_pre, k_post, blk, h):
    rows, lanes = blk.shape
    if h < SUBLANES:
        groups = lambda a: a.reshape(rows // SUBLANES, SUBLANES, lanes)
        later = ((lax.broadcasted_iota(jnp.int32, (1, SUBLANES, 1), 1) // h) % 2) == 1
        other = pltpu.roll(groups(blk), h, 1)
        if 2 * h < SUBLANES:
            other = jnp.where(later, other, pltpu.roll(groups(blk), SUBLANES - h, 1))
        q_pre = (groups(q_pre) * jnp.where(later, other, 1.0)).reshape(rows, lanes)
        k_post = (groups(k_post) * jnp.where(later, 1.0, other)).reshape(rows, lanes)
        blk = (groups(blk) * other).reshape(rows, lanes)
    else:
        (q_lo, q_hi), (k_lo, k_hi), (b_lo, b_hi) = (_split_halves(a, h) for a in (q_pre, k_post, blk))
        q_pre = _join_halves(q_lo, q_hi * b_lo)
        k_post = _join_halves(k_lo * b_hi, k_hi)
        blk = _join_halves(b_lo * b_hi, b_lo * b_hi)
    return q_pre, k_post, blk


def _mixer_front(x, prev, lbl_ref, win_ref, cw_ref, cng_ref, gm_ref, *, n_seq, t_len, valid):
    rows = n_seq * t_len
    log2_c = int(math.log2(t_len))
    assert 1 << log2_c == t_len and t_len % SUBLANES == 0 and valid >= CONV_W - 1
    assert rows % LEVEL_BLOCK == 0 and (t_len <= LEVEL_BLOCK or t_len % LEVEL_BLOCK == 0)
    xb = x.astype(BF16)

    def proj(k):
        return _dot(xb, win_ref[:, k * D_CONV:(k + 1) * D_CONV])

    u = (proj(1) * proj(0)).reshape(n_seq, t_len, D_CONV)
    first = lax.broadcasted_iota(jnp.int32, (1, SUBLANES, 1), 1) == 0

    def delay(a, before):
        r = pltpu.roll(a, 1, 1)
        head = jnp.where(first, before, r[:, :SUBLANES])
        return head if t_len == SUBLANES else jnp.concatenate([head, r[:, SUBLANES:]], axis=1)

    cw = cw_ref[...]
    w0, w1, w2 = cw[0:1], cw[1:2], cw[2:3]
    pair = w0 * delay(u, prev[:, SUBLANES - 1:]) + w1 * u
    pair_before = w0 * prev[:, SUBLANES - 2:SUBLANES - 1] + w1 * prev[:, SUBLANES - 1:]
    conv = (delay(pair, pair_before) + w2 * u).reshape(rows, D_CONV)
    pad8 = t_len - valid
    assert pad8 < SUBLANES
    last8 = u[:, t_len - SUBLANES:]
    next_prev = last8 if pad8 == 0 else pltpu.roll(last8, pad8, 1)
    z = proj(2) * conv
    zz = z * z
    zz_hi = zz.astype(BF16)
    zz_lo = (zz - zz_hi.astype(F32)).astype(BF16)
    gm = gm_ref[...]
    ms = _dot(zz_hi, gm) + _dot(zz_lo, gm)
    y_conv = z * lax.rsqrt(ms + EPS) * cng_ref[...]

    lbl = lbl_ref[...]
    lmax = jnp.max(lbl, axis=0, keepdims=True)
    le = jnp.exp(lbl - lmax)
    lb = le[0:1] / jnp.sum(le, axis=0, keepdims=True)
    q = proj(3)
    qf = q * _sigmoid(q)
    f = lb + (1.0 - lb) * _sigmoid(proj(4))
    kk = 1.0 - f
    v = proj(5)
    gate_in = proj(6)
    gate = gate_in * _sigmoid(gate_in)
    if valid < t_len:
        live = lax.broadcasted_iota(jnp.int32, (rows, 1), 0) % t_len < valid
        f = jnp.where(live, f, 1.0)
        kk = jnp.where(live, kk, 0.0)

    return next_prev, types.SimpleNamespace(qf=qf, kk=kk, f=f, v=v, gate=gate, yc=y_conv.astype(BF16))


def _score_tile(qf, kk, f, t_len, level_rows):
    n_group = SCORE_TILE // SUBLANES
    top = min(t_len, SCORE_TILE)
    sc = [None] * n_group

    def place(level, p, takes):
        for n, i in enumerate(takes):
            hit = level_rows[i] == level
            part = p[n * SUBLANES:(n + 1) * SUBLANES]
            sc[i] = jnp.where(hit, part, 0.0 if sc[i] is None else sc[i])

    place(-1, _dot_nt(qf.astype(BF16), kk.astype(BF16)), range(n_group))
    blocks = [(qf[r:r + LEVEL_BLOCK] * f[r:r + LEVEL_BLOCK], kk[r:r + LEVEL_BLOCK], f[r:r + LEVEL_BLOCK])
              for r in range(0, SCORE_TILE, LEVEL_BLOCK)]
    for lh in range(int(math.log2(top))):
        h = 1 << lh
        if 2 * h > blocks[0][0].shape[0]:
            blocks = [tuple(jnp.concatenate([b[i] for b in blocks], axis=0) for i in range(3))]
        k_all = jnp.concatenate([b[1] for b in blocks], axis=0).astype(BF16)
        if h < SUBLANES:
            q_all = jnp.concatenate([b[0] for b in blocks], axis=0).astype(BF16)
            place(lh, _dot_nt(q_all, k_all), range(n_group))
        else:
            q_later = jnp.concatenate([_split_halves(b[0], h)[1].reshape(-1, DK) for b in blocks], axis=0)
            takes = [i for i in range(n_group) if (i * SUBLANES // h) % 2 == 1]
            place(lh, _dot_nt(q_later.astype(BF16), k_all), takes)
        blocks = [_decay_step(*b, h) for b in blocks]
    q_pre, k_post, blk = (jnp.concatenate([b[i] for b in blocks], axis=0) for i in range(3))
    return jnp.concatenate(sc, axis=0).astype(BF16), q_pre, k_post, blk


def _mixer_back(ops, x, st_ref, hng_ref, wo_ref, g_ref, b_ref, *, n_seq, t_len):
    rows = n_seq * t_len
    assert rows % SCORE_TILE == 0 and (t_len <= SCORE_TILE or t_len == 2 * SCORE_TILE)

    ti = lax.broadcasted_iota(jnp.int32, (SCORE_TILE, SCORE_TILE), 0)
    si = lax.broadcasted_iota(jnp.int32, (SCORE_TILE, SCORE_TILE), 1)
    txs = ti ^ si
    pair_level = jnp.where(ti > si, 0, jnp.where(ti == si, -1, -2))
    for lh in range(1, int(math.log2(SCORE_TILE))):
        pair_level = pair_level + jnp.where((ti > si) & (txs >= (1 << lh)), 1, 0)
    level_rows = [pair_level[i * SUBLANES:(i + 1) * SUBLANES] for i in range(SCORE_TILE // SUBLANES)]

    hng = hng_ref[...]
    y_heads = []
    for hd in range(H_HGRN):
        sl = slice(hd * DK, (hd + 1) * DK)
        vb = ops.v[:, sl].astype(BF16)
        tiles = [_score_tile(ops.qf[r:r + SCORE_TILE, sl], ops.kk[r:r + SCORE_TILE, sl],
                             ops.f[r:r + SCORE_TILE, sl], t_len, level_rows)
                 for r in range(0, rows, SCORE_TILE)]
        o_parts, qc, kc, decay = [], [], [], []
        if t_len <= SCORE_TILE:
            for it, (sc, q_pre, k_post, blk) in enumerate(tiles):
                o_parts.append(_dot(sc, vb[it * SCORE_TILE:(it + 1) * SCORE_TILE]))
                qc.append(q_pre)
                kc.append(k_post)
                decay += [blk[r:r + 1] for r in range(0, SCORE_TILE, t_len)]
        else:
            for g in range(n_seq):
                (sc0, q0, k0, b0), (sc1, q1, k1, b1) = tiles[2 * g], tiles[2 * g + 1]
                cross = _dot_nt(q1.astype(BF16), k0.astype(BF16)).astype(BF16)
                o_parts.append(_dot(sc0, vb[g * t_len:g * t_len + SCORE_TILE]))
                o_parts.append(_dot(jnp.concatenate([cross, sc1], axis=1), vb[g * t_len:(g + 1) * t_len]))
                qc += [q0, q1 * b0[0:1]]
                kc += [k0 * b1[0:1], k1]
                decay.append(b0[0:1] * b1[0:1])
        cat = lambda parts: parts[0] if len(parts) == 1 else jnp.concatenate(parts, axis=0)
        o, qc, kc = cat(o_parts), cat(qc).astype(BF16), cat(kc).astype(BF16)
        inter = []
        for g in range(n_seq):
            rs = slice(g * t_len, (g + 1) * t_len)
            st = st_ref[g, hd]
            inter.append(_dot_nt(qc[rs], st.astype(BF16)))
            st_ref[g, hd] = st * decay[g] + _dot_tn(vb[rs], kc[rs])
        o = o + cat(inter)
        ms_h = jnp.mean(o * o, axis=-1, keepdims=True)
        y_heads.append((o * lax.rsqrt(ms_h + EPS) * hng[:, sl] * ops.gate[:, sl]).astype(BF16))

    y = jnp.concatenate([ops.yc] + y_heads, axis=-1)
    mixed = _dot(y, wo_ref[...])
    return _layer_norm(ALPHA * x + mixed, g_ref[...], b_ref[...])


def _mixer_kernel(*refs, n_seq, t_len, valid, n_tiles, n_cast):
    x_ref, cs_ref, hs_ref, lbl_ref, win_ref, cw_ref, cng_ref, hng_ref, gm_ref, wo_ref, g_ref, b_ref = refs[:12]
    cast_in, (o_ref, cso_ref, hso_ref, *cast_out) = refs[12:12 + n_cast], refs[12 + n_cast:15 + 2 * n_cast]
    st_ref, carry_ref = refs[15 + 2 * n_cast:]
    j = pl.program_id(1)
    assert valid == t_len or n_tiles == 1
    _cast_chunks(cast_in, cast_out)

    @pl.when(j == 0)
    def _():
        carry_ref[...] = cs_ref[...]
        for g in range(n_seq):
            for hd in range(H_HGRN):
                st_ref[g, hd] = hs_ref[g, hd].T

    x = x_ref[...].reshape(n_seq * t_len, D_MODEL)
    next_prev, ops = _mixer_front(x, carry_ref[...], lbl_ref, win_ref, cw_ref, cng_ref, gm_ref,
                                  n_seq=n_seq, t_len=t_len, valid=valid)
    out = _mixer_back(ops, x, st_ref, hng_ref, wo_ref, g_ref, b_ref, n_seq=n_seq, t_len=t_len)
    o_ref[...] = out.reshape(n_seq, t_len, D_MODEL)
    carry_ref[...] = next_prev

    @pl.when(j == n_tiles - 1)
    def _():
        cso_ref[...] = next_prev
        for g in range(n_seq):
            for hd in range(H_HGRN):
                hso_ref[g, hd] = st_ref[g, hd].T


def _mixer_ln(x, conv_state, hgrn_state, valid, n_seq, t_len, lbl, win, cw, cng, hng, gm, wo, g, b, cast=()):
    bsz, seq_len, _ = x.shape
    assert bsz % n_seq == 0 and seq_len % t_len == 0
    n_tiles = seq_len // t_len
    n_groups = bsz // n_seq
    kern = functools.partial(_mixer_kernel, n_seq=n_seq, t_len=t_len, valid=valid - (seq_len - t_len),
                             n_tiles=n_tiles, n_cast=len(cast))
    seq_block = lambda *tail: pl.BlockSpec((n_seq,) + tail, lambda i, j: (i,) + (0,) * len(tail))
    tile_block = pl.BlockSpec((n_seq, t_len, D_MODEL), lambda i, j: (i, j, 0))
    cast_specs, cast_shapes = _cast_specs(cast, n_groups * n_tiles, lambda i, j: i * n_tiles + j)
    y, conv_rows, state, *casted = pl.pallas_call(
        kern,
        grid=(n_groups, n_tiles),
        in_specs=[
            tile_block, seq_block(SUBLANES, D_CONV), seq_block(H_HGRN, DK, DV),
            _const_spec(lbl.shape), _const_spec(win.shape), _const_spec(cw.shape),
            _const_spec(cng.shape), _const_spec(hng.shape), _const_spec(gm.shape),
            _const_spec(wo.shape), _const_spec(g.shape), _const_spec(b.shape),
        ] + cast_specs,
        out_specs=[tile_block, seq_block(SUBLANES, D_CONV), seq_block(H_HGRN, DK, DV)] + cast_specs,
        out_shape=[
            jax.ShapeDtypeStruct(x.shape, F32),
            jax.ShapeDtypeStruct((bsz, SUBLANES, D_CONV), F32),
            jax.ShapeDtypeStruct((bsz, H_HGRN, DK, DV), F32),
        ] + cast_shapes,
        scratch_shapes=[pltpu.VMEM((n_seq, H_HGRN, DV, DK), F32), pltpu.VMEM((n_seq, SUBLANES, D_CONV), F32)],
        compiler_params=pltpu.CompilerParams(
            dimension_semantics=("arbitrary", "arbitrary"), vmem_limit_bytes=VMEM_LIMIT),
    )(x, conv_state, hgrn_state, lbl, win, cw, cng, hng, gm, wo, g, b, *cast)
    return y, conv_rows, state, casted


def kernel(x_prompt, x_sample, state_conv, state_hgrn, lb_logits, ln1_g, ln1_b, ffn1_w_up, ffn1_w_down,
           ln2_g, ln2_b, w_in, conv_w, conv_norm_g, hgrn_norm_g, w_o, ln3_g, ln3_b, ffn2_w_up, ffn2_w_down):
    bsz, seq_len, _ = x_prompt.shape
    dec_b, dec_l, _ = x_sample.shape
    l = 0
    row = lambda a: a[l].reshape(1, -1)
    wup1, wdn1 = ffn1_w_up[l].astype(BF16), ffn1_w_down[l].astype(BF16)
    lane_group = jnp.arange(D_CONV) // (D_CONV // N_CONV_GROUPS)
    gm = ((lane_group[:, None] == lane_group[None, :]) * (N_CONV_GROUPS / D_CONV)).astype(BF16)

    def pad_conv(cs):
        return jnp.pad(cs, ((0, 0), (SUBLANES - (CONV_W - 1), 0), (0, 0)))

    xp, (win, wo) = _ffn_ln(x_prompt.reshape(bsz * seq_len, D_MODEL), wup1, wdn1, row(ln1_g), row(ln1_b),
                            cast=(w_in[l], w_o[l]))
    mix_w = (lb_logits, win, conv_w[l], row(conv_norm_g), row(hgrn_norm_g), gm, wo, row(ln2_g), row(ln2_b))
    zero_conv = jnp.zeros((bsz, SUBLANES, D_CONV), F32)
    zero_hgrn = jnp.zeros((bsz, H_HGRN, DK, DV), F32)
    xp, conv_p, hgrn_p, (wup2, wdn2) = _mixer_ln(
        xp.reshape(bsz, seq_len, D_MODEL), zero_conv, zero_hgrn, seq_len, PROMPT_SEQS, PROMPT_CHUNK, *mix_w,
        cast=(ffn2_w_up[l], ffn2_w_down[l].reshape(D_MODEL, D_FF)))
    wdn2 = wdn2.reshape(D_FF, D_MODEL)
    yp, _ = _ffn_ln(xp.reshape(bsz * seq_len, D_MODEL), wup2, wdn2, row(ln3_g), row(ln3_b))

    xs, _ = _ffn_ln(x_sample.reshape(dec_b * dec_l, D_MODEL), wup1, wdn1, row(ln1_g), row(ln1_b))
    xs = jnp.pad(xs.reshape(dec_b, dec_l, D_MODEL), ((0, 0), (0, SUBLANES - dec_l), (0, 0)))
    xs, conv_s, hgrn_s, _ = _mixer_ln(xs, pad_conv(state_conv[l]), state_hgrn[l], dec_l,
                                      SAMPLE_SEQS, SUBLANES, *mix_w)
    ys, _ = _ffn_ln(xs[:, :dec_l].reshape(dec_b * dec_l, D_MODEL), wup2, wdn2, row(ln3_g), row(ln3_b))

    return (yp.reshape(bsz, seq_len, D_MODEL), ys.reshape(dec_b, dec_l, D_MODEL),
            conv_p[None, :, 1 - CONV_W:], hgrn_p[None], conv_s[None, :, 1 - CONV_W:], hgrn_s[None])
```

```python
import functools
import math
import types

import jax
import jax.numpy as jnp
from jax import lax
from jax.experimental import pallas as pl
from jax.experimental.pallas import tpu as pltpu

D_MODEL = 1024
D_CONV = 512
N_CONV_GROUPS = 8
CONV_W = 3
H_HGRN = 4
DK = 128
DV = 128
HK = H_HGRN * DK
D_FF = 2816
ALPHA = 2.0 ** 0.25
EPS = 1e-5

SUBLANES = 8
LANES = 128
BF16_SUBLANES = 16
MXU_COLS = 256
VMEM_LIMIT = 56 * 1024 * 1024

SCORE_TILE = LANES
LEVEL_BLOCK = 64
FF_TILE = MXU_COLS
FFN_ROWS = 512
LN_CHUNKS = 8
PROMPT_CHUNK = 256
PROMPT_SEQS = 2
SAMPLE_SEQS = 16

F32 = jnp.float32
BF16 = jnp.bfloat16


def _dot(a, b):
    return jnp.dot(a, b, preferred_element_type=F32)


def _dot_nt(a, b):
    return lax.dot_general(a, b, (((1,), (1,)), ((), ())), preferred_element_type=F32)


def _dot_tn(a, b):
    return lax.dot_general(a, b, (((0,), (0,)), ((), ())), preferred_element_type=F32)


def _sigmoid(x):
    return 0.5 * jnp.tanh(0.5 * x) + 0.5


def _layer_norm(z, g, b):
    mu = jnp.mean(z, axis=-1, keepdims=True)
    d = z - mu
    var = jnp.mean(d * d, axis=-1, keepdims=True)
    return d * lax.rsqrt(var + EPS) * g + b


def _always_zero(y):
    rows, cols = y.shape
    bits = pltpu.bitcast(y, jnp.uint32).reshape(rows // SUBLANES, SUBLANES, cols)
    merged = functools.reduce(jnp.bitwise_or, [bits[i] for i in range(rows // SUBLANES)])
    merged = functools.reduce(jnp.bitwise_or, [merged[:, i:i + LANES] for i in range(0, cols, LANES)])
    return ((merged >> 16) >> 16).astype(F32)


def _const_spec(shape):
    nd = len(shape)
    return pl.BlockSpec(shape, lambda *_: (0,) * nd, pipeline_mode=pl.Buffered(1))


def _cast_specs(arrays, layer, n_steps, step_of):
    in_specs, out_specs, shapes = [], [], []
    for a in arrays:
        _, rows, cols = a.shape
        share = next(k for k in range(1, n_steps + 1)
                     if n_steps % k == 0 and rows % (n_steps // k) == 0 and
                     (rows // (n_steps // k)) % BF16_SUBLANES == 0)
        chunk = rows // (n_steps // share)
        in_specs.append(pl.BlockSpec((None, chunk, cols),
                                     lambda *idx, share=share: (layer, step_of(*idx) // share, 0)))
        out_specs.append(pl.BlockSpec((chunk, cols), lambda *idx, share=share: (step_of(*idx) // share, 0)))
        shapes.append(jax.ShapeDtypeStruct((rows, cols), BF16))
    return in_specs, out_specs, shapes


def _cast_chunks(src_refs, dst_refs):
    for src, dst in zip(src_refs, dst_refs):
        dst[...] = src[...].astype(BF16)


def _ffn_ln_kernel(x_ref, wup_ref, wdn_ref, g_ref, b_ref, o_ref, *z_refs, n_tiles):
    s = pl.program_id(0)
    tm = x_ref.shape[0]
    chunk = tm // LN_CHUNKS

    def finish(z_ref, c):
        rows = slice(c * chunk, (c + 1) * chunk)
        y = _layer_norm(z_ref[rows, :], g_ref[...], b_ref[...])
        o_ref[rows, :] = y
        return y

    def step(read_ref, write_ref):
        x = x_ref[...]
        xb = x.astype(BF16)
        acc = jnp.zeros(x.shape, F32)
        for j in range(D_FF // FF_TILE):
            c = j * FF_TILE
            a = _dot(xb, wup_ref[:, c:c + FF_TILE])
            b = _dot(xb, wup_ref[:, D_FF + c:D_FF + c + FF_TILE])
            h = a * _sigmoid(a) * b
            if j < LN_CHUNKS:
                zero = _always_zero(finish(read_ref, j))
                h = jnp.concatenate([h[:SUBLANES] + jnp.concatenate([zero] * (FF_TILE // LANES), axis=1),
                                     h[SUBLANES:]], axis=0)
            acc = acc + _dot(h.astype(BF16), wdn_ref[c:c + FF_TILE, :])
        write_ref[...] = ALPHA * x + 0.5 * acc

    @pl.when(s == 0)
    def _():
        z_refs[1][...] = jnp.zeros(z_refs[1].shape, F32)

    for parity in range(2):
        pl.when((s < n_tiles) & (s % 2 == parity))(functools.partial(step, z_refs[1 - parity], z_refs[parity]))

    @pl.when(s == n_tiles)
    def _():
        for c in range(LN_CHUNKS):
            finish(z_refs[(n_tiles - 1) % 2], c)


def _ffn_ln(x, wup, wdn, g, b):
    m = x.shape[0]
    tm = min(FFN_ROWS, m)
    assert m % tm == 0
    n_tiles = m // tm
    return pl.pallas_call(
        functools.partial(_ffn_ln_kernel, n_tiles=n_tiles),
        grid=(n_tiles + 1,),
        in_specs=[
            pl.BlockSpec((tm, D_MODEL), lambda s: (jnp.minimum(s, n_tiles - 1), 0)),
            _const_spec(wup.shape),
            _const_spec(wdn.shape),
            _const_spec(g.shape),
            _const_spec(b.shape),
        ],
        out_specs=pl.BlockSpec((tm, D_MODEL), lambda s: (jnp.maximum(s - 1, 0), 0)),
        out_shape=jax.ShapeDtypeStruct((m, D_MODEL), F32),
        scratch_shapes=[pltpu.VMEM((tm, D_MODEL), F32), pltpu.VMEM((tm, D_MODEL), F32)],
        compiler_params=pltpu.CompilerParams(
            dimension_semantics=("arbitrary",), vmem_limit_bytes=VMEM_LIMIT),
    )(x, wup, wdn, g, b)


def _split_halves(a, h):
    rows, lanes = a.shape
    return jnp.split(a.reshape(rows // (2 * h), 2, h, lanes), 2, axis=1)


def _join_halves(lo, hi):
    n, _, h, lanes = lo.shape
    return jnp.concatenate([lo, hi], axis=1).reshape(n * 2 * h, lanes)


def _decay_step(q_pre, k_post, blk, h):
    rows, lanes = blk.shape
    if h < SUBLANES:
        groups = lambda a: a.reshape(rows // SUBLANES, SUBLANES, lanes)
        later = ((lax.broadcasted_iota(jnp.int32, (1, SUBLANES, 1), 1) // h) % 2) == 1
        other = pltpu.roll(groups(blk), h, 1)
        if 2 * h < SUBLANES:
            other = jnp.where(later, other, pltpu.roll(groups(blk), SUBLANES - h, 1))
        q_pre = (groups(q_pre) * jnp.where(later, other, 1.0)).reshape(rows, lanes)
        k_post = (groups(k_post) * jnp.where(later, 1.0, other)).reshape(rows, lanes)
        blk = (groups(blk) * other).reshape(rows, lanes)
    else:
        (q_lo, q_hi), (k_lo, k_hi), (b_lo, b_hi) = (_split_halves(a, h) for a in (q_pre, k_post, blk))
        q_pre = _join_halves(q_lo, q_hi * b_lo)
        k_post = _join_halves(k_lo * b_hi, k_hi)
        blk = _join_halves(b_lo * b_hi, b_lo * b_hi)
    return q_pre, k_post, blk


def _mixer_front(x, prev, lbl_ref, win_ref, cw_ref, cng_ref, gm_ref, *, n_seq, t_len, valid):
    rows = n_seq * t_len
    log2_c = int(math.log2(t_len))
    assert 1 << log2_c == t_len and t_len % SUBLANES == 0 and valid >= CONV_W - 1
    assert rows % LEVEL_BLOCK == 0 and (t_len <= LEVEL_BLOCK or t_len % LEVEL_BLOCK == 0)
    xb = x.astype(BF16)

    def proj(k):
        return _dot(xb, win_ref[:, k * D_CONV:(k + 1) * D_CONV])

    u = (proj(1) * proj(0)).reshape(n_seq, t_len, D_CONV)
    first = lax.broadcasted_iota(jnp.int32, (1, SUBLANES, 1), 1) == 0

    def delay(a, before):
        r = pltpu.roll(a, 1, 1)
        head = jnp.where(first, before, r[:, :SUBLANES])
        return head if t_len == SUBLANES else jnp.concatenate([head, r[:, SUBLANES:]], axis=1)

    cw = cw_ref[...]
    w0, w1, w2 = cw[0:1], cw[1:2], cw[2:3]
    pair = w0 * delay(u, prev[:, SUBLANES - 1:]) + w1 * u
    pair_before = w0 * prev[:, SUBLANES - 2:SUBLANES - 1] + w1 * prev[:, SUBLANES - 1:]
    conv = (delay(pair, pair_before) + w2 * u).reshape(rows, D_CONV)
    pad8 = t_len - valid
    assert pad8 < SUBLANES
    last8 = u[:, t_len - SUBLANES:]
    next_prev = last8 if pad8 == 0 else pltpu.roll(last8, pad8, 1)
    z = proj(2) * conv
    zz = z * z
    zz_hi = zz.astype(BF16)
    zz_lo = (zz - zz_hi.astype(F32)).astype(BF16)
    gm = gm_ref[...]
    ms = _dot(zz_hi, gm) + _dot(zz_lo, gm)
    y_conv = z * lax.rsqrt(ms + EPS) * cng_ref[...]

    lbl = lbl_ref[...]
    lmax = jnp.max(lbl, axis=0, keepdims=True)
    le = jnp.exp(lbl - lmax)
    lb = le[0:1] / jnp.sum(le, axis=0, keepdims=True)
    q = proj(3)
    qf = q * _sigmoid(q)
    f = lb + (1.0 - lb) * _sigmoid(proj(4))
    kk = 1.0 - f
    v = proj(5)
    gate_in = proj(6)
    gate = gate_in * _sigmoid(gate_in)
    if valid < t_len:
        live = lax.broadcasted_iota(jnp.int32, (rows, 1), 0) % t_len < valid
        f = jnp.where(live, f, 1.0)
        kk = jnp.where(live, kk, 0.0)

    return next_prev, types.SimpleNamespace(qf=qf, kk=kk, f=f, v=v, gate=gate, yc=y_conv.astype(BF16))


def _score_tile(qf, kk, f, t_len, level_rows):
    n_group = SCORE_TILE // SUBLANES
    top = min(t_len, SCORE_TILE)
    sc = [None] * n_group

    def place(level, p, takes):
        for n, i in enumerate(takes):
            hit = level_rows[i] == level
            part = p[n * SUBLANES:(n + 1) * SUBLANES]
            sc[i] = jnp.where(hit, part, 0.0 if sc[i] is None else sc[i])

    place(-1, _dot_nt(qf.astype(BF16), kk.astype(BF16)), range(n_group))
    blocks = [(qf[r:r + LEVEL_BLOCK] * f[r:r + LEVEL_BLOCK], kk[r:r + LEVEL_BLOCK], f[r:r + LEVEL_BLOCK])
              for r in range(0, SCORE_TILE, LEVEL_BLOCK)]
    for lh in range(int(math.log2(top))):
        h = 1 << lh
        if 2 * h > blocks[0][0].shape[0]:
            blocks = [tuple(jnp.concatenate([b[i] for b in blocks], axis=0) for i in range(3))]
        k_all = jnp.concatenate([b[1] for b in blocks], axis=0).astype(BF16)
        if h < SUBLANES:
            q_all = jnp.concatenate([b[0] for b in blocks], axis=0).astype(BF16)
            place(lh, _dot_nt(q_all, k_all), range(n_group))
        else:
            q_later = jnp.concatenate([_split_halves(b[0], h)[1].reshape(-1, DK) for b in blocks], axis=0)
            takes = [i for i in range(n_group) if (i * SUBLANES // h) % 2 == 1]
            place(lh, _dot_nt(q_later.astype(BF16), k_all), takes)
        blocks = [_decay_step(*b, h) for b in blocks]
    q_pre, k_post, blk = (jnp.concatenate([b[i] for b in blocks], axis=0) for i in range(3))
    return jnp.concatenate(sc, axis=0).astype(BF16), q_pre, k_post, blk


def _mixer_back(ops, x, st_ref, hng_ref, wo_ref, g_ref, b_ref, *, n_seq, t_len):
    rows = n_seq * t_len
    assert rows % SCORE_TILE == 0 and (t_len <= SCORE_TILE or t_len == 2 * SCORE_TILE)

    ti = lax.broadcasted_iota(jnp.int32, (SCORE_TILE, SCORE_TILE), 0)
    si = lax.broadcasted_iota(jnp.int32, (SCORE_TILE, SCORE_TILE), 1)
    txs = ti ^ si
    pair_level = jnp.where(ti > si, 0, jnp.where(ti == si, -1, -2))
    for lh in range(1, int(math.log2(SCORE_TILE))):
        pair_level = pair_level + jnp.where((ti > si) & (txs >= (1 << lh)), 1, 0)
    level_rows = [pair_level[i * SUBLANES:(i + 1) * SUBLANES] for i in range(SCORE_TILE // SUBLANES)]

    hng = hng_ref[...]
    y_heads = []
    for hd in range(H_HGRN):
        sl = slice(hd * DK, (hd + 1) * DK)
        vb = ops.v[:, sl].astype(BF16)
        tiles = [_score_tile(ops.qf[r:r + SCORE_TILE, sl], ops.kk[r:r + SCORE_TILE, sl],
                             ops.f[r:r + SCORE_TILE, sl], t_len, level_rows)
                 for r in range(0, rows, SCORE_TILE)]
        o_parts, qc, kc, decay = [], [], [], []
        if t_len <= SCORE_TILE:
            for it, (sc, q_pre, k_post, blk) in enumerate(tiles):
                o_parts.append(_dot(sc, vb[it * SCORE_TILE:(it + 1) * SCORE_TILE]))
                qc.append(q_pre)
                kc.append(k_post)
                decay += [blk[r:r + 1] for r in range(0, SCORE_TILE, t_len)]
        else:
            for g in range(n_seq):
                (sc0, q0, k0, b0), (sc1, q1, k1, b1) = tiles[2 * g], tiles[2 * g + 1]
                cross = _dot_nt(q1.astype(BF16), k0.astype(BF16)).astype(BF16)
                o_parts.append(_dot(sc0, vb[g * t_len:g * t_len + SCORE_TILE]))
                o_parts.append(_dot(jnp.concatenate([cross, sc1], axis=1), vb[g * t_len:(g + 1) * t_len]))
                qc += [q0, q1 * b0[0:1]]
                kc += [k0 * b1[0:1], k1]
                decay.append(b0[0:1] * b1[0:1])
        cat = lambda parts: parts[0] if len(parts) == 1 else jnp.concatenate(parts, axis=0)
        o, qc, kc = cat(o_parts), cat(qc).astype(BF16), cat(kc).astype(BF16)
        inter = []
        for g in range(n_seq):
            rs = slice(g * t_len, (g + 1) * t_len)
            st = st_ref[g, hd]
            inter.append(_dot_nt(qc[rs], st.astype(BF16)))
            st_ref[g, hd] = st * decay[g] + _dot_tn(vb[rs], kc[rs])
        o = o + cat(inter)
        ms_h = jnp.mean(o * o, axis=-1, keepdims=True)
        y_heads.append((o * lax.rsqrt(ms_h + EPS) * hng[:, sl] * ops.gate[:, sl]).astype(BF16))

    y = jnp.concatenate([ops.yc] + y_heads, axis=-1)
    mixed = _dot(y, wo_ref[...])
    return _layer_norm(ALPHA * x + mixed, g_ref[...], b_ref[...])


def _mixer_kernel(*refs, n_seq, t_len, valid, n_tiles, n_cast):
    x_ref, cs_ref, hs_ref, lbl_ref, win_ref, cw_ref, cng_ref, hng_ref, gm_ref, wo_ref, g_ref, b_ref = refs[:12]
    cast_in, (o_ref, cso_ref, hso_ref, *cast_out) = refs[12:12 + n_cast], refs[12 + n_cast:15 + 2 * n_cast]
    st_ref, carry_ref = refs[15 + 2 * n_cast:]
    j = pl.program_id(1)
    assert valid == t_len or n_tiles == 1
    _cast_chunks(cast_in, cast_out)

    @pl.when(j == 0)
    def _():
        carry_ref[...] = cs_ref[...]
        for g in range(n_seq):
            for hd in range(H_HGRN):
                st_ref[g, hd] = hs_ref[g, hd].T

    x = x_ref[...].reshape(n_seq * t_len, D_MODEL)
    next_prev, ops = _mixer_front(x, carry_ref[...], lbl_ref, win_ref, cw_ref, cng_ref, gm_ref,
                                  n_seq=n_seq, t_len=t_len, valid=valid)
    out = _mixer_back(ops, x, st_ref, hng_ref, wo_ref, g_ref, b_ref, n_seq=n_seq, t_len=t_len)
    o_ref[...] = out.reshape(n_seq, t_len, D_MODEL)
    carry_ref[...] = next_prev

    @pl.when(j == n_tiles - 1)
    def _():
        cso_ref[...] = next_prev
        for g in range(n_seq):
            for hd in range(H_HGRN):
                hso_ref[g, hd] = st_ref[g, hd].T


def _mixer_ln(x, conv_state, hgrn_state, valid, n_seq, t_len, lbl, win, cw, cng, hng, gm, wo, g, b,
              cast=(), layer=0):
    bsz, seq_len, _ = x.shape
    assert bsz % n_seq == 0 and seq_len % t_len == 0
    n_tiles = seq_len // t_len
    n_groups = bsz // n_seq
    kern = functools.partial(_mixer_kernel, n_seq=n_seq, t_len=t_len, valid=valid - (seq_len - t_len),
                             n_tiles=n_tiles, n_cast=len(cast))
    seq_block = lambda *tail: pl.BlockSpec((n_seq,) + tail, lambda i, j: (i,) + (0,) * len(tail))
    tile_block = pl.BlockSpec((n_seq, t_len, D_MODEL), lambda i, j: (i, j, 0))
    cast_in, cast_out, cast_shapes = _cast_specs(cast, layer, n_groups * n_tiles, lambda i, j: i * n_tiles + j)
    y, conv_rows, state, *casted = pl.pallas_call(
        kern,
        grid=(n_groups, n_tiles),
        in_specs=[
            tile_block, seq_block(SUBLANES, D_CONV), seq_block(H_HGRN, DK, DV),
            _const_spec(lbl.shape), _const_spec(win.shape), _const_spec(cw.shape),
            _const_spec(cng.shape), _const_spec(hng.shape), _const_spec(gm.shape),
            _const_spec(wo.shape), _const_spec(g.shape), _const_spec(b.shape),
        ] + cast_in,
        out_specs=[tile_block, seq_block(SUBLANES, D_CONV), seq_block(H_HGRN, DK, DV)] + cast_out,
        out_shape=[
            jax.ShapeDtypeStruct(x.shape, F32),
            jax.ShapeDtypeStruct((bsz, SUBLANES, D_CONV), F32),
            jax.ShapeDtypeStruct((bsz, H_HGRN, DK, DV), F32),
        ] + cast_shapes,
        scratch_shapes=[pltpu.VMEM((n_seq, H_HGRN, DV, DK), F32), pltpu.VMEM((n_seq, SUBLANES, D_CONV), F32)],
        compiler_params=pltpu.CompilerParams(
            dimension_semantics=("arbitrary", "arbitrary"), vmem_limit_bytes=VMEM_LIMIT),
    )(x, conv_state, hgrn_state, lbl, win, cw, cng, hng, gm, wo, g, b, *cast)
    return y, conv_rows, state, casted


def kernel(x_prompt, x_sample, state_conv, state_hgrn, lb_logits, ln1_g, ln1_b, ffn1_w_up, ffn1_w_down,
           ln2_g, ln2_b, w_in, conv_w, conv_norm_g, hgrn_norm_g, w_o, ln3_g, ln3_b, ffn2_w_up, ffn2_w_down):
    bsz, seq_len, _ = x_prompt.shape
    dec_b, dec_l, _ = x_sample.shape
    l = 0
    row = lambda a: a[l].reshape(1, -1)
    wup1, wdn1 = ffn1_w_up[l].astype(BF16), ffn1_w_down[l].astype(BF16)
    win, wo = w_in[l].astype(BF16), w_o[l].astype(BF16)
    lane_group = jnp.arange(D_CONV) // (D_CONV // N_CONV_GROUPS)
    gm = ((lane_group[:, None] == lane_group[None, :]) * (N_CONV_GROUPS / D_CONV)).astype(BF16)

    def pad_conv(cs):
        return jnp.pad(cs, ((0, 0), (SUBLANES - (CONV_W - 1), 0), (0, 0)))

    xp = _ffn_ln(x_prompt.reshape(bsz * seq_len, D_MODEL), wup1, wdn1, row(ln1_g), row(ln1_b))
    mix_w = (lb_logits, win, conv_w[l], row(conv_norm_g), row(hgrn_norm_g), gm, wo, row(ln2_g), row(ln2_b))
    zero_conv = jnp.zeros((bsz, SUBLANES, D_CONV), F32)
    zero_hgrn = jnp.zeros((bsz, H_HGRN, DK, DV), F32)
    xp, conv_p, hgrn_p, (wup2, wdn2) = _mixer_ln(
        xp.reshape(bsz, seq_len, D_MODEL), zero_conv, zero_hgrn, seq_len, PROMPT_SEQS, PROMPT_CHUNK, *mix_w,
        cast=(ffn2_w_up, ffn2_w_down), layer=l)
    yp = _ffn_ln(xp.reshape(bsz * seq_len, D_MODEL), wup2, wdn2, row(ln3_g), row(ln3_b))

    xs = _ffn_ln(x_sample.reshape(dec_b * dec_l, D_MODEL), wup1, wdn1, row(ln1_g), row(ln1_b))
    xs = jnp.pad(xs.reshape(dec_b, dec_l, D_MODEL), ((0, 0), (0, SUBLANES - dec_l), (0, 0)))
    xs, conv_s, hgrn_s, _ = _mixer_ln(xs, pad_conv(state_conv[l]), state_hgrn[l], dec_l,
                                      SAMPLE_SEQS, SUBLANES, *mix_w)
    ys = _ffn_ln(xs[:, :dec_l].reshape(dec_b * dec_l, D_MODEL), wup2, wdn2, row(ln3_g), row(ln3_b))

    return (yp.reshape(bsz, seq_len, D_MODEL), ys.reshape(dec_b, dec_l, D_MODEL),
            conv_p[None, :, 1 - CONV_W:], hgrn_p[None], conv_s[None, :, 1 - CONV_W:], hgrn_s[None])
```

```python
import functools
import math
import types

import jax
import jax.numpy as jnp
from jax import lax
from jax.experimental import pallas as pl
from jax.experimental.pallas import tpu as pltpu

D_MODEL = 1024
D_CONV = 512
N_CONV_GROUPS = 8
CONV_W = 3
H_HGRN = 4
DK = 128
DV = 128
HK = H_HGRN * DK
D_FF = 2816
ALPHA = 2.0 ** 0.25
EPS = 1e-5

SUBLANES = 8
LANES = 128
BF16_SUBLANES = 16
MXU_COLS = 256
VMEM_LIMIT = 56 * 1024 * 1024

SCORE_TILE = LANES
LEVEL_BLOCK = 64
FF_TILE = MXU_COLS
FFN_ROWS = 512
LN_CHUNKS = 8
PROMPT_CHUNK = 256
PROMPT_SEQS = 2
SAMPLE_SEQS = 16

F32 = jnp.float32
BF16 = jnp.bfloat16


def _dot(a, b):
    return jnp.dot(a, b, preferred_element_type=F32)


def _dot_nt(a, b):
    return lax.dot_general(a, b, (((1,), (1,)), ((), ())), preferred_element_type=F32)


def _dot_tn(a, b):
    return lax.dot_general(a, b, (((0,), (0,)), ((), ())), preferred_element_type=F32)


def _sigmoid(x):
    return 0.5 * jnp.tanh(0.5 * x) + 0.5


def _layer_norm(z, g, b):
    mu = jnp.mean(z, axis=-1, keepdims=True)
    d = z - mu
    var = jnp.mean(d * d, axis=-1, keepdims=True)
    return d * lax.rsqrt(var + EPS) * g + b


def _always_zero(y):
    rows, cols = y.shape
    bits = pltpu.bitcast(y, jnp.uint32).reshape(rows // SUBLANES, SUBLANES, cols)
    merged = functools.reduce(jnp.bitwise_or, [bits[i] for i in range(rows // SUBLANES)])
    merged = functools.reduce(jnp.bitwise_or, [merged[:, i:i + LANES] for i in range(0, cols, LANES)])
    return ((merged >> 16) >> 16).astype(F32)


def _const_spec(shape):
    nd = len(shape)
    return pl.BlockSpec(shape, lambda *_: (0,) * nd, pipeline_mode=pl.Buffered(1))


def _cast_specs(arrays, layer, n_steps, step_of):
    in_specs, out_specs, shapes = [], [], []
    for a in arrays:
        _, rows, cols = a.shape
        share = next(k for k in range(1, n_steps + 1)
                     if n_steps % k == 0 and rows % (n_steps // k) == 0 and
                     (rows // (n_steps // k)) % BF16_SUBLANES == 0)
        chunk = rows // (n_steps // share)
        in_specs.append(pl.BlockSpec((None, chunk, cols),
                                     lambda *idx, share=share: (layer, step_of(*idx) // share, 0)))
        out_specs.append(pl.BlockSpec((chunk, cols), lambda *idx, share=share: (step_of(*idx) // share, 0)))
        shapes.append(jax.ShapeDtypeStruct((rows, cols), BF16))
    return in_specs, out_specs, shapes


def _cast_chunks(src_refs, dst_refs):
    for src, dst in zip(src_refs, dst_refs):
        dst[...] = src[...].astype(BF16)


def _ffn_ln_kernel(x_ref, xe_ref, wup_ref, wdn_ref, g_ref, b_ref, o_ref, oe_ref, *z_refs, n_tiles):
    s = pl.program_id(0)
    tm = x_ref.shape[0]
    chunk = tm // LN_CHUNKS

    def finish(z_ref, dst_ref, c):
        rows = slice(c * chunk, (c + 1) * chunk)
        y = _layer_norm(z_ref[rows, :], g_ref[...], b_ref[...])
        dst_ref[rows, :] = y
        return y

    def step(src_ref, read_ref, write_ref):
        x = src_ref[...]
        xb = x.astype(BF16)
        acc = jnp.zeros(x.shape, F32)
        for j in range(D_FF // FF_TILE):
            c = j * FF_TILE
            a = _dot(xb, wup_ref[:, c:c + FF_TILE])
            b = _dot(xb, wup_ref[:, D_FF + c:D_FF + c + FF_TILE])
            h = a * _sigmoid(a) * b
            if j < LN_CHUNKS:
                zero = _always_zero(finish(read_ref, o_ref, j))
                h = jnp.concatenate([h[:SUBLANES] + jnp.concatenate([zero] * (FF_TILE // LANES), axis=1),
                                     h[SUBLANES:]], axis=0)
            acc = acc + _dot(h.astype(BF16), wdn_ref[c:c + FF_TILE, :])
        write_ref[...] = ALPHA * x + 0.5 * acc

    @pl.when(s == 0)
    def _():
        z_refs[1][...] = jnp.zeros(z_refs[1].shape, F32)

    for parity in range(2):
        pl.when((s < n_tiles) & (s % 2 == parity))(
            functools.partial(step, x_ref, z_refs[1 - parity], z_refs[parity]))
    pl.when(s == n_tiles)(
        functools.partial(step, xe_ref, z_refs[(n_tiles - 1) % 2], z_refs[n_tiles % 2]))

    @pl.when(s == n_tiles + 1)
    def _():
        for c in range(LN_CHUNKS):
            finish(z_refs[n_tiles % 2], oe_ref, c)


def _ffn_ln(x, x_extra, wup, wdn, g, b):
    m = x.shape[0]
    tm = x_extra.shape[0]
    assert m % tm == 0
    n_tiles = m // tm
    tile = lambda s: jnp.clip(s, 0, n_tiles - 1)
    return pl.pallas_call(
        functools.partial(_ffn_ln_kernel, n_tiles=n_tiles),
        grid=(n_tiles + 2,),
        in_specs=[
            pl.BlockSpec((tm, D_MODEL), lambda s: (tile(s), 0)),
            pl.BlockSpec((tm, D_MODEL), lambda s: (0, 0)),
            _const_spec(wup.shape),
            _const_spec(wdn.shape),
            _const_spec(g.shape),
            _const_spec(b.shape),
        ],
        out_specs=[pl.BlockSpec((tm, D_MODEL), lambda s: (tile(s - 1), 0)),
                   pl.BlockSpec((tm, D_MODEL), lambda s: (0, 0))],
        out_shape=[jax.ShapeDtypeStruct((m, D_MODEL), F32), jax.ShapeDtypeStruct((tm, D_MODEL), F32)],
        scratch_shapes=[pltpu.VMEM((tm, D_MODEL), F32), pltpu.VMEM((tm, D_MODEL), F32)],
        compiler_params=pltpu.CompilerParams(
            dimension_semantics=("arbitrary",), vmem_limit_bytes=VMEM_LIMIT),
    )(x, x_extra, wup, wdn, g, b)


def _split_halves(a, h):
    rows, lanes = a.shape
    return jnp.split(a.reshape(rows // (2 * h), 2, h, lanes), 2, axis=1)


def _join_halves(lo, hi):
    n, _, h, lanes = lo.shape
    return jnp.concatenate([lo, hi], axis=1).reshape(n * 2 * h, lanes)


def _decay_step(q_pre, k_post, blk, h):
    rows, lanes = blk.shape
    if h < SUBLANES:
        groups = lambda a: a.reshape(rows // SUBLANES, SUBLANES, lanes)
        later = ((lax.broadcasted_iota(jnp.int32, (1, SUBLANES, 1), 1) // h) % 2) == 1
        other = pltpu.roll(groups(blk), h, 1)
        if 2 * h < SUBLANES:
            other = jnp.where(later, other, pltpu.roll(groups(blk), SUBLANES - h, 1))
        q_pre = (groups(q_pre) * jnp.where(later, other, 1.0)).reshape(rows, lanes)
        k_post = (groups(k_post) * jnp.where(later, 1.0, other)).reshape(rows, lanes)
        blk = (groups(blk) * other).reshape(rows, lanes)
    else:
        (q_lo, q_hi), (k_lo, k_hi), (b_lo, b_hi) = (_split_halves(a, h) for a in (q_pre, k_post, blk))
        q_pre = _join_halves(q_lo, q_hi * b_lo)
        k_post = _join_halves(k_lo * b_hi, k_hi)
        blk = _join_halves(b_lo * b_hi, b_lo * b_hi)
    return q_pre, k_post, blk


def _mixer_front(x, prev, lbl_ref, win_ref, cw_ref, cng_ref, gm_ref, *, n_seq, t_len, valid):
    rows = n_seq * t_len
    log2_c = int(math.log2(t_len))
    assert 1 << log2_c == t_len and t_len % SUBLANES == 0 and valid >= CONV_W - 1
    assert rows % LEVEL_BLOCK == 0 and (t_len <= LEVEL_BLOCK or t_len % LEVEL_BLOCK == 0)
    xb = x.astype(BF16)

    def proj(k):
        return _dot(xb, win_ref[:, k * D_CONV:(k + 1) * D_CONV])

    u = (proj(1) * proj(0)).reshape(n_seq, t_len, D_CONV)
    first = lax.broadcasted_iota(jnp.int32, (1, SUBLANES, 1), 1) == 0

    def delay(a, before):
        r = pltpu.roll(a, 1, 1)
        head = jnp.where(first, before, r[:, :SUBLANES])
        return head if t_len == SUBLANES else jnp.concatenate([head, r[:, SUBLANES:]], axis=1)

    cw = cw_ref[...]
    w0, w1, w2 = cw[0:1], cw[1:2], cw[2:3]
    pair = w0 * delay(u, prev[:, SUBLANES - 1:]) + w1 * u
    pair_before = w0 * prev[:, SUBLANES - 2:SUBLANES - 1] + w1 * prev[:, SUBLANES - 1:]
    conv = (delay(pair, pair_before) + w2 * u).reshape(rows, D_CONV)
    pad8 = t_len - valid
    assert pad8 < SUBLANES
    last8 = u[:, t_len - SUBLANES:]
    next_prev = last8 if pad8 == 0 else pltpu.roll(last8, pad8, 1)
    z = proj(2) * conv
    zz = z * z
    zz_hi = zz.astype(BF16)
    zz_lo = (zz - zz_hi.astype(F32)).astype(BF16)
    gm = gm_ref[...]
    ms = _dot(zz_hi, gm) + _dot(zz_lo, gm)
    y_conv = z * lax.rsqrt(ms + EPS) * cng_ref[...]

    lbl = lbl_ref[...]
    lmax = jnp.max(lbl, axis=0, keepdims=True)
    le = jnp.exp(lbl - lmax)
    lb = le[0:1] / jnp.sum(le, axis=0, keepdims=True)
    q = proj(3)
    qf = q * _sigmoid(q)
    f = lb + (1.0 - lb) * _sigmoid(proj(4))
    kk = 1.0 - f
    v = proj(5)
    gate_in = proj(6)
    gate = gate_in * _sigmoid(gate_in)
    if valid < t_len:
        live = lax.broadcasted_iota(jnp.int32, (rows, 1), 0) % t_len < valid
        f = jnp.where(live, f, 1.0)
        kk = jnp.where(live, kk, 0.0)

    return next_prev, types.SimpleNamespace(qf=qf, kk=kk, f=f, v=v, gate=gate, yc=y_conv.astype(BF16))


def _score_tile(qf, kk, f, t_len, level_rows):
    n_group = SCORE_TILE // SUBLANES
    top = min(t_len, SCORE_TILE)
    sc = [None] * n_group

    def place(level, p, takes):
        for n, i in enumerate(takes):
            hit = level_rows[i] == level
            part = p[n * SUBLANES:(n + 1) * SUBLANES]
            sc[i] = jnp.where(hit, part, 0.0 if sc[i] is None else sc[i])

    place(-1, _dot_nt(qf.astype(BF16), kk.astype(BF16)), range(n_group))
    blocks = [(qf[r:r + LEVEL_BLOCK] * f[r:r + LEVEL_BLOCK], kk[r:r + LEVEL_BLOCK], f[r:r + LEVEL_BLOCK])
              for r in range(0, SCORE_TILE, LEVEL_BLOCK)]
    for lh in range(int(math.log2(top))):
        h = 1 << lh
        if 2 * h > blocks[0][0].shape[0]:
            blocks = [tuple(jnp.concatenate([b[i] for b in blocks], axis=0) for i in range(3))]
        k_all = jnp.concatenate([b[1] for b in blocks], axis=0).astype(BF16)
        if h < SUBLANES:
            q_all = jnp.concatenate([b[0] for b in blocks], axis=0).astype(BF16)
            place(lh, _dot_nt(q_all, k_all), range(n_group))
        else:
            q_later = jnp.concatenate([_split_halves(b[0], h)[1].reshape(-1, DK) for b in blocks], axis=0)
            takes = [i for i in range(n_group) if (i * SUBLANES // h) % 2 == 1]
            place(lh, _dot_nt(q_later.astype(BF16), k_all), takes)
        blocks = [_decay_step(*b, h) for b in blocks]
    q_pre, k_post, blk = (jnp.concatenate([b[i] for b in blocks], axis=0) for i in range(3))
    return jnp.concatenate(sc, axis=0).astype(BF16), q_pre, k_post, blk


def _mixer_back(ops, x, st_ref, hng_ref, wo_ref, g_ref, b_ref, *, n_seq, t_len):
    rows = n_seq * t_len
    assert rows % SCORE_TILE == 0 and (t_len <= SCORE_TILE or t_len == 2 * SCORE_TILE)

    ti = lax.broadcasted_iota(jnp.int32, (SCORE_TILE, SCORE_TILE), 0)
    si = lax.broadcasted_iota(jnp.int32, (SCORE_TILE, SCORE_TILE), 1)
    txs = ti ^ si
    pair_level = jnp.where(ti > si, 0, jnp.where(ti == si, -1, -2))
    for lh in range(1, int(math.log2(SCORE_TILE))):
        pair_level = pair_level + jnp.where((ti > si) & (txs >= (1 << lh)), 1, 0)
    level_rows = [pair_level[i * SUBLANES:(i + 1) * SUBLANES] for i in range(SCORE_TILE // SUBLANES)]

    hng = hng_ref[...]
    y_heads = []
    for hd in range(H_HGRN):
        sl = slice(hd * DK, (hd + 1) * DK)
        vb = ops.v[:, sl].astype(BF16)
        tiles = [_score_tile(ops.qf[r:r + SCORE_TILE, sl], ops.kk[r:r + SCORE_TILE, sl],
                             ops.f[r:r + SCORE_TILE, sl], t_len, level_rows)
                 for r in range(0, rows, SCORE_TILE)]
        o_parts, qc, kc, decay = [], [], [], []
        if t_len <= SCORE_TILE:
            for it, (sc, q_pre, k_post, blk) in enumerate(tiles):
                o_parts.append(_dot(sc, vb[it * SCORE_TILE:(it + 1) * SCORE_TILE]))
                qc.append(q_pre)
                kc.append(k_post)
                decay += [blk[r:r + 1] for r in range(0, SCORE_TILE, t_len)]
        else:
            for g in range(n_seq):
                (sc0, q0, k0, b0), (sc1, q1, k1, b1) = tiles[2 * g], tiles[2 * g + 1]
                cross = _dot_nt(q1.astype(BF16), k0.astype(BF16)).astype(BF16)
                o_parts.append(_dot(sc0, vb[g * t_len:g * t_len + SCORE_TILE]))
                o_parts.append(_dot(jnp.concatenate([cross, sc1], axis=1), vb[g * t_len:(g + 1) * t_len]))
                qc += [q0, q1 * b0[0:1]]
                kc += [k0 * b1[0:1], k1]
                decay.append(b0[0:1] * b1[0:1])
        cat = lambda parts: parts[0] if len(parts) == 1 else jnp.concatenate(parts, axis=0)
        o, qc, kc = cat(o_parts), cat(qc).astype(BF16), cat(kc).astype(BF16)
        inter = []
        for g in range(n_seq):
            rs = slice(g * t_len, (g + 1) * t_len)
            st = st_ref[g, hd]
            inter.append(_dot_nt(qc[rs], st.astype(BF16)))
            st_ref[g, hd] = st * decay[g] + _dot_tn(vb[rs], kc[rs])
        o = o + cat(inter)
        ms_h = jnp.mean(o * o, axis=-1, keepdims=True)
        y_heads.append((o * lax.rsqrt(ms_h + EPS) * hng[:, sl] * ops.gate[:, sl]).astype(BF16))

    y = jnp.concatenate([ops.yc] + y_heads, axis=-1)
    mixed = _dot(y, wo_ref[...])
    return _layer_norm(ALPHA * x + mixed, g_ref[...], b_ref[...])


def _mixer_kernel(*refs, n_seq, t_len, valid, n_tiles, n_cast):
    x_ref, cs_ref, hs_ref, lbl_ref, win_ref, cw_ref, cng_ref, hng_ref, gm_ref, wo_ref, g_ref, b_ref = refs[:12]
    cast_in, (o_ref, cso_ref, hso_ref, *cast_out) = refs[12:12 + n_cast], refs[12 + n_cast:15 + 2 * n_cast]
    st_ref, carry_ref = refs[15 + 2 * n_cast:]
    j = pl.program_id(1)
    assert valid == t_len or n_tiles == 1
    _cast_chunks(cast_in, cast_out)

    @pl.when(j == 0)
    def _():
        carry_ref[...] = cs_ref[...]
        for g in range(n_seq):
            for hd in range(H_HGRN):
                st_ref[g, hd] = hs_ref[g, hd].T

    x = x_ref[...].reshape(n_seq * t_len, D_MODEL)
    next_prev, ops = _mixer_front(x, carry_ref[...], lbl_ref, win_ref, cw_ref, cng_ref, gm_ref,
                                  n_seq=n_seq, t_len=t_len, valid=valid)
    out = _mixer_back(ops, x, st_ref, hng_ref, wo_ref, g_ref, b_ref, n_seq=n_seq, t_len=t_len)
    o_ref[...] = out.reshape(n_seq, t_len, D_MODEL)
    carry_ref[...] = next_prev

    @pl.when(j == n_tiles - 1)
    def _():
        cso_ref[...] = next_prev
        for g in range(n_seq):
            for hd in range(H_HGRN):
                hso_ref[g, hd] = st_ref[g, hd].T


def _mixer_ln(x, conv_state, hgrn_state, valid, n_seq, t_len, lbl, win, cw, cng, hng, gm, wo, g, b,
              cast=(), layer=0):
    bsz, seq_len, _ = x.shape
    assert bsz % n_seq == 0 and seq_len % t_len == 0
    n_tiles = seq_len // t_len
    n_groups = bsz // n_seq
    kern = functools.partial(_mixer_kernel, n_seq=n_seq, t_len=t_len, valid=valid - (seq_len - t_len),
                             n_tiles=n_tiles, n_cast=len(cast))
    seq_block = lambda *tail: pl.BlockSpec((n_seq,) + tail, lambda i, j: (i,) + (0,) * len(tail))
    tile_block = pl.BlockSpec((n_seq, t_len, D_MODEL), lambda i, j: (i, j, 0))
    cast_in, cast_out, cast_shapes = _cast_specs(cast, layer, n_groups * n_tiles, lambda i, j: i * n_tiles + j)
    y, conv_rows, state, *casted = pl.pallas_call(
        kern,
        grid=(n_groups, n_tiles),
        in_specs=[
            tile_block, seq_block(SUBLANES, D_CONV), seq_block(H_HGRN, DK, DV),
            _const_spec(lbl.shape), _const_spec(win.shape), _const_spec(cw.shape),
            _const_spec(cng.shape), _const_spec(hng.shape), _const_spec(gm.shape),
            _const_spec(wo.shape), _const_spec(g.shape), _const_spec(b.shape),
        ] + cast_in,
        out_specs=[tile_block, seq_block(SUBLANES, D_CONV), seq_block(H_HGRN, DK, DV)] + cast_out,
        out_shape=[
            jax.ShapeDtypeStruct(x.shape, F32),
            jax.ShapeDtypeStruct((bsz, SUBLANES, D_CONV), F32),
            jax.ShapeDtypeStruct((bsz, H_HGRN, DK, DV), F32),
        ] + cast_shapes,
        scratch_shapes=[pltpu.VMEM((n_seq, H_HGRN, DV, DK), F32), pltpu.VMEM((n_seq, SUBLANES, D_CONV), F32)],
        compiler_params=pltpu.CompilerParams(
            dimension_semantics=("arbitrary", "arbitrary"), vmem_limit_bytes=VMEM_LIMIT),
    )(x, conv_state, hgrn_state, lbl, win, cw, cng, hng, gm, wo, g, b, *cast)
    return y, conv_rows, state, casted


def kernel(x_prompt, x_sample, state_conv, state_hgrn, lb_logits, ln1_g, ln1_b, ffn1_w_up, ffn1_w_down,
           ln2_g, ln2_b, w_in, conv_w, conv_norm_g, hgrn_norm_g, w_o, ln3_g, ln3_b, ffn2_w_up, ffn2_w_down):
    bsz, seq_len, _ = x_prompt.shape
    dec_b, dec_l, _ = x_sample.shape
    l = 0
    row = lambda a: a[l].reshape(1, -1)
    wup1, wdn1 = ffn1_w_up[l].astype(BF16), ffn1_w_down[l].astype(BF16)
    win, wo = w_in[l].astype(BF16), w_o[l].astype(BF16)
    lane_group = jnp.arange(D_CONV) // (D_CONV // N_CONV_GROUPS)
    gm = ((lane_group[:, None] == lane_group[None, :]) * (N_CONV_GROUPS / D_CONV)).astype(BF16)

    def pad_conv(cs):
        return jnp.pad(cs, ((0, 0), (SUBLANES - (CONV_W - 1), 0), (0, 0)))

    assert dec_b * dec_l == FFN_ROWS
    xp, xs = _ffn_ln(x_prompt.reshape(bsz * seq_len, D_MODEL), x_sample.reshape(dec_b * dec_l, D_MODEL),
                     wup1, wdn1, row(ln1_g), row(ln1_b))
    mix_w = (lb_logits, win, conv_w[l], row(conv_norm_g), row(hgrn_norm_g), gm, wo, row(ln2_g), row(ln2_b))

    zero_conv = jnp.zeros((bsz, SUBLANES, D_CONV), F32)
    zero_hgrn = jnp.zeros((bsz, H_HGRN, DK, DV), F32)
    xp, conv_p, hgrn_p, (wup2, wdn2) = _mixer_ln(
        xp.reshape(bsz, seq_len, D_MODEL), zero_conv, zero_hgrn, seq_len, PROMPT_SEQS, PROMPT_CHUNK, *mix_w,
        cast=(ffn2_w_up, ffn2_w_down), layer=l)

    xs = jnp.pad(xs.reshape(dec_b, dec_l, D_MODEL), ((0, 0), (0, SUBLANES - dec_l), (0, 0)))
    xs, conv_s, hgrn_s, _ = _mixer_ln(xs, pad_conv(state_conv[l]), state_hgrn[l], dec_l,
                                      SAMPLE_SEQS, SUBLANES, *mix_w)

    yp, ys = _ffn_ln(xp.reshape(bsz * seq_len, D_MODEL), xs[:, :dec_l].reshape(dec_b * dec_l, D_MODEL),
                     wup2, wdn2, row(ln3_g), row(ln3_b))

    return (yp.reshape(bsz, seq_len, D_MODEL), ys.reshape(dec_b, dec_l, D_MODEL),
            conv_p[None, :, 1 - CONV_W:], hgrn_p[None], conv_s[None, :, 1 - CONV_W:], hgrn_s[None])
```

```python
import functools
import math
import types

import jax
import jax.numpy as jnp
from jax import lax
from jax.experimental import pallas as pl
from jax.experimental.pallas import tpu as pltpu

D_MODEL = 1024
D_CONV = 512
N_CONV_GROUPS = 8
CONV_W = 3
H_HGRN = 4
DK = 128
DV = 128
HK = H_HGRN * DK
D_FF = 2816
ALPHA = 2.0 ** 0.25
EPS = 1e-5

SUBLANES = 8
LANES = 128
BF16_SUBLANES = 16
MXU_COLS = 256
VMEM_LIMIT = 56 * 1024 * 1024

SCORE_TILE = LANES
LEVEL_BLOCK = 64
FF_TILE = MXU_COLS
FFN_ROWS = 512
LN_CHUNKS = 8
PROMPT_CHUNK = 256
PROMPT_SEQS = 4
SAMPLE_SEQS = 16

F32 = jnp.float32
BF16 = jnp.bfloat16


def _dot(a, b):
    return jnp.dot(a, b, preferred_element_type=F32)


def _dot_nt(a, b):
    return lax.dot_general(a, b, (((1,), (1,)), ((), ())), preferred_element_type=F32)


def _dot_tn(a, b):
    return lax.dot_general(a, b, (((0,), (0,)), ((), ())), preferred_element_type=F32)


def _silu(x):
    h = 0.5 * x
    return h * jnp.tanh(h) + h


def _layer_norm(z, g, b):
    mu = jnp.mean(z, axis=-1, keepdims=True)
    d = z - mu
    var = jnp.mean(d * d, axis=-1, keepdims=True)
    return d * lax.rsqrt(var + EPS) * g + b


def _always_zero(y):
    rows, cols = y.shape
    bits = pltpu.bitcast(y, jnp.uint32).reshape(rows // SUBLANES, SUBLANES, cols)
    merged = functools.reduce(jnp.bitwise_or, [bits[i] for i in range(rows // SUBLANES)])
    merged = functools.reduce(jnp.bitwise_or, [merged[:, i:i + LANES] for i in range(0, cols, LANES)])
    return ((merged >> 16) >> 16).astype(F32)


def _const_spec(shape):
    nd = len(shape)
    return pl.BlockSpec(shape, lambda *_: (0,) * nd, pipeline_mode=pl.Buffered(1))


def _cast_specs(arrays, layer, n_steps, step_of):
    in_specs, out_specs, shapes = [], [], []
    for a in arrays:
        _, rows, cols = a.shape
        share = next(k for k in range(1, n_steps + 1)
                     if n_steps % k == 0 and rows % (n_steps // k) == 0 and
                     (rows // (n_steps // k)) % BF16_SUBLANES == 0)
        chunk = rows // (n_steps // share)
        in_specs.append(pl.BlockSpec((None, chunk, cols),
                                     lambda *idx, share=share: (layer, step_of(*idx) // share, 0)))
        out_specs.append(pl.BlockSpec((chunk, cols), lambda *idx, share=share: (step_of(*idx) // share, 0)))
        shapes.append(jax.ShapeDtypeStruct((rows, cols), BF16))
    return in_specs, out_specs, shapes


def _cast_chunks(src_refs, dst_refs):
    for src, dst in zip(src_refs, dst_refs):
        dst[...] = src[...].astype(BF16)


def _ffn_ln_kernel(x_ref, xe_ref, wup_ref, wdn_ref, g_ref, b_ref, o_ref, oe_ref, *z_refs, n_tiles):
    s = pl.program_id(0)
    tm = x_ref.shape[0]
    chunk = tm // LN_CHUNKS

    def finish(z_ref, dst_ref, c):
        rows = slice(c * chunk, (c + 1) * chunk)
        y = _layer_norm(z_ref[rows, :], g_ref[...], b_ref[...])
        dst_ref[rows, :] = y
        return y

    def step(src_ref, read_ref, write_ref):
        x = src_ref[...]
        xb = x.astype(BF16)
        acc = jnp.zeros(x.shape, F32)
        for j in range(D_FF // FF_TILE):
            c = j * FF_TILE
            a = _dot(xb, wup_ref[:, c:c + FF_TILE])
            b = _dot(xb, wup_ref[:, D_FF + c:D_FF + c + FF_TILE])
            h = _silu(a) * b
            if j < LN_CHUNKS:
                zero = _always_zero(finish(read_ref, o_ref, j))
                h = jnp.concatenate([h[:SUBLANES] + jnp.concatenate([zero] * (FF_TILE // LANES), axis=1),
                                     h[SUBLANES:]], axis=0)
            acc = acc + _dot(h.astype(BF16), wdn_ref[c:c + FF_TILE, :])
        write_ref[...] = ALPHA * x + 0.5 * acc

    @pl.when(s == 0)
    def _():
        z_refs[1][...] = jnp.zeros(z_refs[1].shape, F32)

    for parity in range(2):
        pl.when((s < n_tiles) & (s % 2 == parity))(
            functools.partial(step, x_ref, z_refs[1 - parity], z_refs[parity]))
    pl.when(s == n_tiles)(
        functools.partial(step, xe_ref, z_refs[(n_tiles - 1) % 2], z_refs[n_tiles % 2]))

    @pl.when(s == n_tiles + 1)
    def _():
        for c in range(LN_CHUNKS):
            finish(z_refs[n_tiles % 2], oe_ref, c)


def _ffn_ln(x, x_extra, wup, wdn, g, b):
    m = x.shape[0]
    tm = x_extra.shape[0]
    assert m % tm == 0
    n_tiles = m // tm
    tile = lambda s: jnp.clip(s, 0, n_tiles - 1)
    return pl.pallas_call(
        functools.partial(_ffn_ln_kernel, n_tiles=n_tiles),
        grid=(n_tiles + 2,),
        in_specs=[
            pl.BlockSpec((tm, D_MODEL), lambda s: (tile(s), 0)),
            pl.BlockSpec((tm, D_MODEL), lambda s: (0, 0)),
            _const_spec(wup.shape),
            _const_spec(wdn.shape),
            _const_spec(g.shape),
            _const_spec(b.shape),
        ],
        out_specs=[pl.BlockSpec((tm, D_MODEL), lambda s: (tile(s - 1), 0)),
                   pl.BlockSpec((tm, D_MODEL), lambda s: (0, 0))],
        out_shape=[jax.ShapeDtypeStruct((m, D_MODEL), F32), jax.ShapeDtypeStruct((tm, D_MODEL), F32)],
        scratch_shapes=[pltpu.VMEM((tm, D_MODEL), F32), pltpu.VMEM((tm, D_MODEL), F32)],
        compiler_params=pltpu.CompilerParams(
            dimension_semantics=("arbitrary",), vmem_limit_bytes=VMEM_LIMIT),
    )(x, x_extra, wup, wdn, g, b)


def _split_halves(a, h):
    rows, lanes = a.shape
    return jnp.split(a.reshape(rows // (2 * h), 2, h, lanes), 2, axis=1)


def _join_halves(lo, hi):
    n, _, h, lanes = lo.shape
    return jnp.concatenate([lo, hi], axis=1).reshape(n * 2 * h, lanes)


def _decay_step(q_pre, k_post, blk, h):
    rows, lanes = blk.shape
    if h < SUBLANES:
        groups = lambda a: a.reshape(rows // SUBLANES, SUBLANES, lanes)
        later = ((lax.broadcasted_iota(jnp.int32, (1, SUBLANES, 1), 1) // h) % 2) == 1
        other = pltpu.roll(groups(blk), h, 1)
        if 2 * h < SUBLANES:
            other = jnp.where(later, other, pltpu.roll(groups(blk), SUBLANES - h, 1))
        q_pre = (groups(q_pre) * jnp.where(later, other, 1.0)).reshape(rows, lanes)
        k_post = (groups(k_post) * jnp.where(later, 1.0, other)).reshape(rows, lanes)
        blk = (groups(blk) * other).reshape(rows, lanes)
    else:
        (q_lo, q_hi), (k_lo, k_hi), (b_lo, b_hi) = (_split_halves(a, h) for a in (q_pre, k_post, blk))
        q_pre = _join_halves(q_lo, q_hi * b_lo)
        k_post = _join_halves(k_lo * b_hi, k_hi)
        blk = _join_halves(b_lo * b_hi, b_lo * b_hi)
    return q_pre, k_post, blk


def _mixer_front(x, prev, lbl_ref, win_ref, cw_ref, cng_ref, gm_ref, *, n_seq, t_len, valid):
    rows = n_seq * t_len
    log2_c = int(math.log2(t_len))
    assert 1 << log2_c == t_len and t_len % SUBLANES == 0 and valid >= CONV_W - 1
    assert rows % LEVEL_BLOCK == 0 and (t_len <= LEVEL_BLOCK or t_len % LEVEL_BLOCK == 0)
    xb = x.astype(BF16)

    def proj(k):
        return _dot(xb, win_ref[:, k * D_CONV:(k + 1) * D_CONV])

    u = (proj(1) * proj(0)).reshape(n_seq, t_len, D_CONV)
    first = lax.broadcasted_iota(jnp.int32, (1, SUBLANES, 1), 1) == 0

    def delay(a, before):
        r = pltpu.roll(a, 1, 1)
        head = jnp.where(first, before, r[:, :SUBLANES])
        return head if t_len == SUBLANES else jnp.concatenate([head, r[:, SUBLANES:]], axis=1)

    cw = cw_ref[...]
    w0, w1, w2 = cw[0:1], cw[1:2], cw[2:3]
    pair = w0 * delay(u, prev[:, SUBLANES - 1:]) + w1 * u
    pair_before = w0 * prev[:, SUBLANES - 2:SUBLANES - 1] + w1 * prev[:, SUBLANES - 1:]
    conv = (delay(pair, pair_before) + w2 * u).reshape(rows, D_CONV)
    pad8 = t_len - valid
    assert pad8 < SUBLANES
    last8 = u[:, t_len - SUBLANES:]
    next_prev = last8 if pad8 == 0 else pltpu.roll(last8, pad8, 1)
    z = proj(2) * conv
    zz = z * z
    zz_hi = zz.astype(BF16)
    zz_lo = (zz - zz_hi.astype(F32)).astype(BF16)
    gm = gm_ref[...]
    ms = _dot(zz_hi, gm) + _dot(zz_lo, gm)
    y_conv = z * lax.rsqrt(ms + EPS) * cng_ref[...]

    lbl = lbl_ref[...]
    lmax = jnp.max(lbl, axis=0, keepdims=True)
    le = jnp.exp(lbl - lmax)
    lb = le[0:1] / jnp.sum(le, axis=0, keepdims=True)
    q = proj(3)
    qf = _silu(q)
    f = 0.5 * (1.0 + lb) + (0.5 * (1.0 - lb)) * jnp.tanh(0.5 * proj(4))
    kk = 1.0 - f
    v = proj(5)
    gate_in = proj(6)
    gate = _silu(gate_in)
    if valid < t_len:
        live = lax.broadcasted_iota(jnp.int32, (rows, 1), 0) % t_len < valid
        f = jnp.where(live, f, 1.0)
        kk = jnp.where(live, kk, 0.0)

    return next_prev, types.SimpleNamespace(qf=qf, kk=kk, f=f, v=v, gate=gate, yc=y_conv.astype(BF16))


def _score_tile(qf, kk, f, t_len, level_rows):
    n_group = SCORE_TILE // SUBLANES
    top = min(t_len, SCORE_TILE)
    sc = [None] * n_group

    def place(level, p, takes):
        for n, i in enumerate(takes):
            hit = level_rows[i] == level
            part = p[n * SUBLANES:(n + 1) * SUBLANES]
            sc[i] = jnp.where(hit, part, 0.0 if sc[i] is None else sc[i])

    place(-1, _dot_nt(qf.astype(BF16), kk.astype(BF16)), range(n_group))
    blocks = [(qf[r:r + LEVEL_BLOCK] * f[r:r + LEVEL_BLOCK], kk[r:r + LEVEL_BLOCK], f[r:r + LEVEL_BLOCK])
              for r in range(0, SCORE_TILE, LEVEL_BLOCK)]
    for lh in range(int(math.log2(top))):
        h = 1 << lh
        if 2 * h > blocks[0][0].shape[0]:
            blocks = [tuple(jnp.concatenate([b[i] for b in blocks], axis=0) for i in range(3))]
        k_all = jnp.concatenate([b[1] for b in blocks], axis=0).astype(BF16)
        if h < SUBLANES:
            q_all = jnp.concatenate([b[0] for b in blocks], axis=0).astype(BF16)
            place(lh, _dot_nt(q_all, k_all), range(n_group))
        else:
            q_later = jnp.concatenate([_split_halves(b[0], h)[1].reshape(-1, DK) for b in blocks], axis=0)
            takes = [i for i in range(n_group) if (i * SUBLANES // h) % 2 == 1]
            place(lh, _dot_nt(q_later.astype(BF16), k_all), takes)
        blocks = [_decay_step(*b, h) for b in blocks]
    q_pre, k_post, blk = (jnp.concatenate([b[i] for b in blocks], axis=0) for i in range(3))
    return jnp.concatenate(sc, axis=0).astype(BF16), q_pre, k_post, blk


def _mixer_back(ops, x, st_ref, hng_ref, wo_ref, g_ref, b_ref, *, n_seq, t_len):
    rows = n_seq * t_len
    assert rows % SCORE_TILE == 0 and (t_len <= SCORE_TILE or t_len == 2 * SCORE_TILE)

    ti = lax.broadcasted_iota(jnp.int32, (SCORE_TILE, SCORE_TILE), 0)
    si = lax.broadcasted_iota(jnp.int32, (SCORE_TILE, SCORE_TILE), 1)
    txs = ti ^ si
    pair_level = jnp.where(ti > si, 0, jnp.where(ti == si, -1, -2))
    for lh in range(1, int(math.log2(SCORE_TILE))):
        pair_level = pair_level + jnp.where((ti > si) & (txs >= (1 << lh)), 1, 0)
    level_rows = [pair_level[i * SUBLANES:(i + 1) * SUBLANES] for i in range(SCORE_TILE // SUBLANES)]

    hng = hng_ref[...]
    y_heads = []
    for hd in range(H_HGRN):
        sl = slice(hd * DK, (hd + 1) * DK)
        vb = ops.v[:, sl].astype(BF16)
        tiles = [_score_tile(ops.qf[r:r + SCORE_TILE, sl], ops.kk[r:r + SCORE_TILE, sl],
                             ops.f[r:r + SCORE_TILE, sl], t_len, level_rows)
                 for r in range(0, rows, SCORE_TILE)]
        o_parts, qc, kc, decay = [], [], [], []
        if t_len <= SCORE_TILE:
            for it, (sc, q_pre, k_post, blk) in enumerate(tiles):
                o_parts.append(_dot(sc, vb[it * SCORE_TILE:(it + 1) * SCORE_TILE]))
                qc.append(q_pre)
                kc.append(k_post)
                decay += [blk[r:r + 1] for r in range(0, SCORE_TILE, t_len)]
        else:
            for g in range(n_seq):
                (sc0, q0, k0, b0), (sc1, q1, k1, b1) = tiles[2 * g], tiles[2 * g + 1]
                cross = _dot_nt(q1.astype(BF16), k0.astype(BF16)).astype(BF16)
                o_parts.append(_dot(sc0, vb[g * t_len:g * t_len + SCORE_TILE]))
                o_parts.append(_dot(jnp.concatenate([cross, sc1], axis=1), vb[g * t_len:(g + 1) * t_len]))
                qc += [q0, q1 * b0[0:1]]
                kc += [k0 * b1[0:1], k1]
                decay.append(b0[0:1] * b1[0:1])
        cat = lambda parts: parts[0] if len(parts) == 1 else jnp.concatenate(parts, axis=0)
        o, qc, kc = cat(o_parts), cat(qc).astype(BF16), cat(kc).astype(BF16)
        inter = []
        for g in range(n_seq):
            rs = slice(g * t_len, (g + 1) * t_len)
            st = st_ref[g, hd]
            inter.append(_dot_nt(qc[rs], st.astype(BF16)))
            st_ref[g, hd] = st * decay[g] + _dot_tn(vb[rs], kc[rs])
        o = o + cat(inter)
        ms_h = jnp.mean(o * o, axis=-1, keepdims=True)
        y_heads.append((o * lax.rsqrt(ms_h + EPS) * hng[:, sl] * ops.gate[:, sl]).astype(BF16))

    y = jnp.concatenate([ops.yc] + y_heads, axis=-1)
    mixed = _dot(y, wo_ref[...])
    return _layer_norm(ALPHA * x + mixed, g_ref[...], b_ref[...])


def _mixer_kernel(*refs, n_seq, t_len, valid, n_tiles, n_cast):
    x_ref, cs_ref, hs_ref, lbl_ref, win_ref, cw_ref, cng_ref, hng_ref, gm_ref, wo_ref, g_ref, b_ref = refs[:12]
    cast_in, (o_ref, cso_ref, hso_ref, *cast_out) = refs[12:12 + n_cast], refs[12 + n_cast:15 + 2 * n_cast]
    st_ref, carry_ref = refs[15 + 2 * n_cast:]
    j = pl.program_id(1)
    assert valid == t_len or n_tiles == 1
    _cast_chunks(cast_in, cast_out)

    @pl.when(j == 0)
    def _():
        carry_ref[...] = cs_ref[...]
        for g in range(n_seq):
            for hd in range(H_HGRN):
                st_ref[g, hd] = hs_ref[g, hd].T

    x = x_ref[...].reshape(n_seq * t_len, D_MODEL)
    next_prev, ops = _mixer_front(x, carry_ref[...], lbl_ref, win_ref, cw_ref, cng_ref, gm_ref,
                                  n_seq=n_seq, t_len=t_len, valid=valid)
    out = _mixer_back(ops, x, st_ref, hng_ref, wo_ref, g_ref, b_ref, n_seq=n_seq, t_len=t_len)
    o_ref[...] = out.reshape(n_seq, t_len, D_MODEL)
    carry_ref[...] = next_prev

    @pl.when(j == n_tiles - 1)
    def _():
        cso_ref[...] = next_prev
        for g in range(n_seq):
            for hd in range(H_HGRN):
                hso_ref[g, hd] = st_ref[g, hd].T


def _mixer_ln(x, conv_state, hgrn_state, valid, n_seq, t_len, lbl, win, cw, cng, hng, gm, wo, g, b,
              cast=(), layer=0):
    bsz, seq_len, _ = x.shape
    assert bsz % n_seq == 0 and seq_len % t_len == 0
    n_tiles = seq_len // t_len
    n_groups = bsz // n_seq
    kern = functools.partial(_mixer_kernel, n_seq=n_seq, t_len=t_len, valid=valid - (seq_len - t_len),
                             n_tiles=n_tiles, n_cast=len(cast))
    seq_block = lambda *tail: pl.BlockSpec((n_seq,) + tail, lambda i, j: (i,) + (0,) * len(tail))
    tile_block = pl.BlockSpec((n_seq, t_len, D_MODEL), lambda i, j: (i, j, 0))
    cast_in, cast_out, cast_shapes = _cast_specs(cast, layer, n_groups * n_tiles, lambda i, j: i * n_tiles + j)
    y, conv_rows, state, *casted = pl.pallas_call(
        kern,
        grid=(n_groups, n_tiles),
        in_specs=[
            tile_block, seq_block(SUBLANES, D_CONV), seq_block(H_HGRN, DK, DV),
            _const_spec(lbl.shape), _const_spec(win.shape), _const_spec(cw.shape),
            _const_spec(cng.shape), _const_spec(hng.shape), _const_spec(gm.shape),
            _const_spec(wo.shape), _const_spec(g.shape), _const_spec(b.shape),
        ] + cast_in,
        out_specs=[tile_block, seq_block(SUBLANES, D_CONV), seq_block(H_HGRN, DK, DV)] + cast_out,
        out_shape=[
            jax.ShapeDtypeStruct(x.shape, F32),
            jax.ShapeDtypeStruct((bsz, SUBLANES, D_CONV), F32),
            jax.ShapeDtypeStruct((bsz, H_HGRN, DK, DV), F32),
        ] + cast_shapes,
        scratch_shapes=[pltpu.VMEM((n_seq, H_HGRN, DV, DK), F32), pltpu.VMEM((n_seq, SUBLANES, D_CONV), F32)],
        compiler_params=pltpu.CompilerParams(
            dimension_semantics=("arbitrary", "arbitrary"), vmem_limit_bytes=VMEM_LIMIT),
    )(x, conv_state, hgrn_state, lbl, win, cw, cng, hng, gm, wo, g, b, *cast)
    return y, conv_rows, state, casted


def kernel(x_prompt, x_sample, state_conv, state_hgrn, lb_logits, ln1_g, ln1_b, ffn1_w_up, ffn1_w_down,
           ln2_g, ln2_b, w_in, conv_w, conv_norm_g, hgrn_norm_g, w_o, ln3_g, ln3_b, ffn2_w_up, ffn2_w_down):
    bsz, seq_len, _ = x_prompt.shape
    dec_b, dec_l, _ = x_sample.shape
    l = 0
    row = lambda a: a[l].reshape(1, -1)
    wup1, wdn1 = ffn1_w_up[l].astype(BF16), ffn1_w_down[l].astype(BF16)
    win, wo = w_in[l].astype(BF16), w_o[l].astype(BF16)
    lane_group = jnp.arange(D_CONV) // (D_CONV // N_CONV_GROUPS)
    gm = ((lane_group[:, None] == lane_group[None, :]) * (N_CONV_GROUPS / D_CONV)).astype(BF16)

    def pad_conv(cs):
        return jnp.pad(cs, ((0, 0), (SUBLANES - (CONV_W - 1), 0), (0, 0)))

    assert dec_b * dec_l == FFN_ROWS
    xp, xs = _ffn_ln(x_prompt.reshape(bsz * seq_len, D_MODEL), x_sample.reshape(dec_b * dec_l, D_MODEL),
                     wup1, wdn1, row(ln1_g), row(ln1_b))
    mix_w = (lb_logits, win, conv_w[l], row(conv_norm_g), row(hgrn_norm_g), gm, wo, row(ln2_g), row(ln2_b))

    zero_conv = jnp.zeros((bsz, SUBLANES, D_CONV), F32)
    zero_hgrn = jnp.zeros((bsz, H_HGRN, DK, DV), F32)
    xp, conv_p, hgrn_p, (wup2, wdn2) = _mixer_ln(
        xp.reshape(bsz, seq_len, D_MODEL), zero_conv, zero_hgrn, seq_len, PROMPT_SEQS, PROMPT_CHUNK, *mix_w,
        cast=(ffn2_w_up, ffn2_w_down), layer=l)

    xs = jnp.pad(xs.reshape(dec_b, dec_l, D_MODEL), ((0, 0), (0, SUBLANES - dec_l), (0, 0)))
    xs, conv_s, hgrn_s, _ = _mixer_ln(xs, pad_conv(state_conv[l]), state_hgrn[l], dec_l,
                                      SAMPLE_SEQS, SUBLANES, *mix_w)

    yp, ys = _ffn_ln(xp.reshape(bsz * seq_len, D_MODEL), xs[:, :dec_l].reshape(dec_b * dec_l, D_MODEL),
                     wup2, wdn2, row(ln3_g), row(ln3_b))

    return (yp.reshape(bsz, seq_len, D_MODEL), ys.reshape(dec_b, dec_l, D_MODEL),
            conv_p[None, :, 1 - CONV_W:], hgrn_p[None], conv_s[None, :, 1 - CONV_W:], hgrn_s[None])
```

```python
import functools
import math
import types

import jax
import jax.numpy as jnp
from jax import lax
from jax.experimental import pallas as pl
from jax.experimental.pallas import tpu as pltpu

D_MODEL = 1024
D_CONV = 512
N_CONV_GROUPS = 8
CONV_W = 3
H_HGRN = 4
DK = 128
DV = 128
HK = H_HGRN * DK
D_FF = 2816
ALPHA = 2.0 ** 0.25
EPS = 1e-5

SUBLANES = 8
LANES = 128
BF16_SUBLANES = 16
MXU_COLS = 256
VMEM_LIMIT = 56 * 1024 * 1024

SCORE_TILE = LANES
LEVEL_BLOCK = 64
FF_TILE = MXU_COLS
FFN_ROWS = 512
LN_CHUNKS = 8
PROMPT_CHUNK = 256
PROMPT_SEQS = 4
SAMPLE_SEQS = 16

F32 = jnp.float32
BF16 = jnp.bfloat16


def _dot(a, b):
    return jnp.dot(a, b, preferred_element_type=F32)


def _dot_nt(a, b):
    return lax.dot_general(a, b, (((1,), (1,)), ((), ())), preferred_element_type=F32)


def _dot_tn(a, b):
    return lax.dot_general(a, b, (((0,), (0,)), ((), ())), preferred_element_type=F32)


def _sigmoid(x):
    return 0.5 * jnp.tanh(0.5 * x) + 0.5


def _layer_norm(z, g, b):
    mu = jnp.mean(z, axis=-1, keepdims=True)
    d = z - mu
    var = jnp.mean(d * d, axis=-1, keepdims=True)
    return d * lax.rsqrt(var + EPS) * g + b


def _always_zero(y):
    rows, cols = y.shape
    bits = pltpu.bitcast(y, jnp.uint32).reshape(rows // SUBLANES, SUBLANES, cols)
    merged = functools.reduce(jnp.bitwise_or, [bits[i] for i in range(rows // SUBLANES)])
    merged = functools.reduce(jnp.bitwise_or, [merged[:, i:i + LANES] for i in range(0, cols, LANES)])
    return ((merged >> 16) >> 16).astype(F32)


def _const_spec(shape):
    nd = len(shape)
    return pl.BlockSpec(shape, lambda *_: (0,) * nd, pipeline_mode=pl.Buffered(1))


def _cast_specs(arrays, layer, n_steps, step_of):
    in_specs, out_specs, shapes = [], [], []
    for a in arrays:
        _, rows, cols = a.shape
        share = next(k for k in range(1, n_steps + 1)
                     if n_steps % k == 0 and rows % (n_steps // k) == 0 and
                     (rows // (n_steps // k)) % BF16_SUBLANES == 0)
        chunk = rows // (n_steps // share)
        in_specs.append(pl.BlockSpec((None, chunk, cols),
                                     lambda *idx, share=share: (layer, step_of(*idx) // share, 0)))
        out_specs.append(pl.BlockSpec((chunk, cols), lambda *idx, share=share: (step_of(*idx) // share, 0)))
        shapes.append(jax.ShapeDtypeStruct((rows, cols), BF16))
    return in_specs, out_specs, shapes


def _cast_chunks(src_refs, dst_refs):
    for src, dst in zip(src_refs, dst_refs):
        dst[...] = src[...].astype(BF16)


def _ffn_ln_kernel(x_ref, xe_ref, wup_ref, wdn_ref, g_ref, b_ref, o_ref, oe_ref, *z_refs, n_tiles):
    s = pl.program_id(0)
    tm = x_ref.shape[0]
    chunk = tm // LN_CHUNKS

    def finish(z_ref, dst_ref, c):
        rows = slice(c * chunk, (c + 1) * chunk)
        y = _layer_norm(z_ref[rows, :], g_ref[...], b_ref[...])
        dst_ref[rows, :] = y
        return y

    def step(src_ref, read_ref, write_ref):
        x = src_ref[...]
        xb = x.astype(BF16)
        acc = jnp.zeros(x.shape, F32)
        for j in range(D_FF // FF_TILE):
            c = j * FF_TILE
            a = _dot(xb, wup_ref[:, c:c + FF_TILE])
            b = _dot(xb, wup_ref[:, D_FF + c:D_FF + c + FF_TILE])
            h = a * _sigmoid(a) * b
            if j < LN_CHUNKS:
                zero = _always_zero(finish(read_ref, o_ref, j))
                h = jnp.concatenate([h[:SUBLANES] + jnp.concatenate([zero] * (FF_TILE // LANES), axis=1),
                                     h[SUBLANES:]], axis=0)
            acc = acc + _dot(h.astype(BF16), wdn_ref[c:c + FF_TILE, :])
        write_ref[...] = ALPHA * x + 0.5 * acc

    @pl.when(s == 0)
    def _():
        z_refs[1][...] = jnp.zeros(z_refs[1].shape, F32)

    for parity in range(2):
        pl.when((s < n_tiles) & (s % 2 == parity))(
            functools.partial(step, x_ref, z_refs[1 - parity], z_refs[parity]))
    pl.when(s == n_tiles)(
        functools.partial(step, xe_ref, z_refs[(n_tiles - 1) % 2], z_refs[n_tiles % 2]))

    @pl.when(s == n_tiles + 1)
    def _():
        for c in range(LN_CHUNKS):
            finish(z_refs[n_tiles % 2], oe_ref, c)


def _ffn_ln(x, x_extra, wup, wdn, g, b):
    m = x.shape[0]
    tm = x_extra.shape[0]
    assert m % tm == 0
    n_tiles = m // tm
    tile = lambda s: jnp.clip(s, 0, n_tiles - 1)
    return pl.pallas_call(
        functools.partial(_ffn_ln_kernel, n_tiles=n_tiles),
        grid=(n_tiles + 2,),
        in_specs=[
            pl.BlockSpec((tm, D_MODEL), lambda s: (tile(s), 0)),
            pl.BlockSpec((tm, D_MODEL), lambda s: (0, 0)),
            _const_spec(wup.shape),
            _const_spec(wdn.shape),
            _const_spec(g.shape),
            _const_spec(b.shape),
        ],
        out_specs=[pl.BlockSpec((tm, D_MODEL), lambda s: (tile(s - 1), 0)),
                   pl.BlockSpec((tm, D_MODEL), lambda s: (0, 0))],
        out_shape=[jax.ShapeDtypeStruct((m, D_MODEL), F32), jax.ShapeDtypeStruct((tm, D_MODEL), F32)],
        scratch_shapes=[pltpu.VMEM((tm, D_MODEL), F32), pltpu.VMEM((tm, D_MODEL), F32)],
        compiler_params=pltpu.CompilerParams(
            dimension_semantics=("arbitrary",), vmem_limit_bytes=VMEM_LIMIT),
    )(x, x_extra, wup, wdn, g, b)


def _split_halves(a, h):
    rows, lanes = a.shape
    return jnp.split(a.reshape(rows // (2 * h), 2, h, lanes), 2, axis=1)


def _join_halves(lo, hi):
    n, _, h, lanes = lo.shape
    return jnp.concatenate([lo, hi], axis=1).reshape(n * 2 * h, lanes)


def _decay_step(q_pre, k_post, blk, h):
    rows, lanes = blk.shape
    if h < SUBLANES:
        groups = lambda a: a.reshape(rows // SUBLANES, SUBLANES, lanes)
        later = ((lax.broadcasted_iota(jnp.int32, (1, SUBLANES, 1), 1) // h) % 2) == 1
        other = pltpu.roll(groups(blk), h, 1)
        if 2 * h < SUBLANES:
            other = jnp.where(later, other, pltpu.roll(groups(blk), SUBLANES - h, 1))
        q_pre = (groups(q_pre) * jnp.where(later, other, 1.0)).reshape(rows, lanes)
        k_post = (groups(k_post) * jnp.where(later, 1.0, other)).reshape(rows, lanes)
        blk = (groups(blk) * other).reshape(rows, lanes)
    else:
        (q_lo, q_hi), (k_lo, k_hi), (b_lo, b_hi) = (_split_halves(a, h) for a in (q_pre, k_post, blk))
        q_pre = _join_halves(q_lo, q_hi * b_lo)
        k_post = _join_halves(k_lo * b_hi, k_hi)
        blk = _join_halves(b_lo * b_hi, b_lo * b_hi)
    return q_pre, k_post, blk


def _mixer_front(x, prev, lbl_ref, win_ref, cw_ref, cng_ref, gm_ref, *, n_seq, t_len, valid):
    rows = n_seq * t_len
    log2_c = int(math.log2(t_len))
    assert 1 << log2_c == t_len and t_len % SUBLANES == 0 and valid >= CONV_W - 1
    assert rows % LEVEL_BLOCK == 0 and (t_len <= LEVEL_BLOCK or t_len % LEVEL_BLOCK == 0)
    xb = x.astype(BF16)

    def proj(k):
        return _dot(xb, win_ref[:, k * D_CONV:(k + 1) * D_CONV])

    u = (proj(1) * proj(0)).reshape(n_seq, t_len, D_CONV)
    first = lax.broadcasted_iota(jnp.int32, (1, SUBLANES, 1), 1) == 0

    def delay(a, before):
        r = pltpu.roll(a, 1, 1)
        head = jnp.where(first, before, r[:, :SUBLANES])
        return head if t_len == SUBLANES else jnp.concatenate([head, r[:, SUBLANES:]], axis=1)

    cw = cw_ref[...]
    w0, w1, w2 = cw[0:1], cw[1:2], cw[2:3]
    pair = w0 * delay(u, prev[:, SUBLANES - 1:]) + w1 * u
    pair_before = w0 * prev[:, SUBLANES - 2:SUBLANES - 1] + w1 * prev[:, SUBLANES - 1:]
    conv = (delay(pair, pair_before) + w2 * u).reshape(rows, D_CONV)
    pad8 = t_len - valid
    assert pad8 < SUBLANES
    last8 = u[:, t_len - SUBLANES:]
    next_prev = last8 if pad8 == 0 else pltpu.roll(last8, pad8, 1)
    z = proj(2) * conv
    zz = z * z
    zz_hi = zz.astype(BF16)
    zz_lo = (zz - zz_hi.astype(F32)).astype(BF16)
    gm = gm_ref[...]
    ms = _dot(zz_hi, gm) + _dot(zz_lo, gm)
    y_conv = z * lax.rsqrt(ms + EPS) * cng_ref[...]

    lbl = lbl_ref[...]
    lmax = jnp.max(lbl, axis=0, keepdims=True)
    le = jnp.exp(lbl - lmax)
    lb = le[0:1] / jnp.sum(le, axis=0, keepdims=True)
    q = proj(3)
    qf = q * _sigmoid(q)
    f = lb + (1.0 - lb) * _sigmoid(proj(4))
    kk = 1.0 - f
    v = proj(5)
    gate_in = proj(6)
    gate = gate_in * _sigmoid(gate_in)
    if valid < t_len:
        live = lax.broadcasted_iota(jnp.int32, (rows, 1), 0) % t_len < valid
        f = jnp.where(live, f, 1.0)
        kk = jnp.where(live, kk, 0.0)

    return next_prev, types.SimpleNamespace(qf=qf, kk=kk, f=f, v=v, gate=gate, yc=y_conv.astype(BF16))


def _score_tile(qf, kk, f, t_len, level_rows):
    n_group = SCORE_TILE // SUBLANES
    top = min(t_len, SCORE_TILE)
    sc = [None] * n_group

    def place(level, p, takes):
        for n, i in enumerate(takes):
            hit = level_rows[i] == level
            part = p[n * SUBLANES:(n + 1) * SUBLANES]
            sc[i] = jnp.where(hit, part, 0.0 if sc[i] is None else sc[i])

    place(-1, _dot_nt(qf.astype(BF16), kk.astype(BF16)), range(n_group))
    blocks = [(qf[r:r + LEVEL_BLOCK] * f[r:r + LEVEL_BLOCK], kk[r:r + LEVEL_BLOCK], f[r:r + LEVEL_BLOCK])
              for r in range(0, SCORE_TILE, LEVEL_BLOCK)]
    for lh in range(int(math.log2(top))):
        h = 1 << lh
        if 2 * h > blocks[0][0].shape[0]:
            blocks = [tuple(jnp.concatenate([b[i] for b in blocks], axis=0) for i in range(3))]
        k_all = jnp.concatenate([b[1] for b in blocks], axis=0).astype(BF16)
        if h < SUBLANES:
            q_all = jnp.concatenate([b[0] for b in blocks], axis=0).astype(BF16)
            place(lh, _dot_nt(q_all, k_all), range(n_group))
        else:
            q_later = jnp.concatenate([_split_halves(b[0], h)[1].reshape(-1, DK) for b in blocks], axis=0)
            takes = [i for i in range(n_group) if (i * SUBLANES // h) % 2 == 1]
            place(lh, _dot_nt(q_later.astype(BF16), k_all), takes)
        blocks = [_decay_step(*b, h) for b in blocks]
    q_pre, k_post, blk = (jnp.concatenate([b[i] for b in blocks], axis=0) for i in range(3))
    return jnp.concatenate(sc, axis=0).astype(BF16), q_pre, k_post, blk


def _mixer_back(ops, x, st_ref, hng_ref, wo_ref, g_ref, b_ref, *, n_seq, t_len):
    rows = n_seq * t_len
    assert rows % SCORE_TILE == 0 and (t_len <= SCORE_TILE or t_len == 2 * SCORE_TILE)

    ti = lax.broadcasted_iota(jnp.int32, (SCORE_TILE, SCORE_TILE), 0)
    si = lax.broadcasted_iota(jnp.int32, (SCORE_TILE, SCORE_TILE), 1)
    txs = ti ^ si
    pair_level = jnp.where(ti > si, 0, jnp.where(ti == si, -1, -2))
    for lh in range(1, int(math.log2(SCORE_TILE))):
        pair_level = pair_level + jnp.where((ti > si) & (txs >= (1 << lh)), 1, 0)
    level_rows = [pair_level[i * SUBLANES:(i + 1) * SUBLANES] for i in range(SCORE_TILE // SUBLANES)]

    hng = hng_ref[...]
    y_heads = []
    for hd in range(H_HGRN):
        sl = slice(hd * DK, (hd + 1) * DK)
        vb = ops.v[:, sl].astype(BF16)
        tiles = [_score_tile(ops.qf[r:r + SCORE_TILE, sl], ops.kk[r:r + SCORE_TILE, sl],
                             ops.f[r:r + SCORE_TILE, sl], t_len, level_rows)
                 for r in range(0, rows, SCORE_TILE)]
        o_parts, qc, kc, decay = [], [], [], []
        if t_len <= SCORE_TILE:
            for it, (sc, q_pre, k_post, blk) in enumerate(tiles):
                o_parts.append(_dot(sc, vb[it * SCORE_TILE:(it + 1) * SCORE_TILE]))
                qc.append(q_pre)
                kc.append(k_post)
                decay += [blk[r:r + 1] for r in range(0, SCORE_TILE, t_len)]
        else:
            for g in range(n_seq):
                (sc0, q0, k0, b0), (sc1, q1, k1, b1) = tiles[2 * g], tiles[2 * g + 1]
                cross = _dot_nt(q1.astype(BF16), k0.astype(BF16)).astype(BF16)
                o_parts.append(_dot(sc0, vb[g * t_len:g * t_len + SCORE_TILE]))
                o_parts.append(_dot(jnp.concatenate([cross, sc1], axis=1), vb[g * t_len:(g + 1) * t_len]))
                qc += [q0, q1 * b0[0:1]]
                kc += [k0 * b1[0:1], k1]
                decay.append(b0[0:1] * b1[0:1])
        cat = lambda parts: parts[0] if len(parts) == 1 else jnp.concatenate(parts, axis=0)
        o, qc, kc = cat(o_parts), cat(qc).astype(BF16), cat(kc).astype(BF16)
        inter = []
        for g in range(n_seq):
            rs = slice(g * t_len, (g + 1) * t_len)
            st = st_ref[g, hd]
            inter.append(_dot_nt(qc[rs], st.astype(BF16)))
            st_ref[g, hd] = st * decay[g] + _dot_tn(vb[rs], kc[rs])
        o = o + cat(inter)
        ms_h = jnp.mean(o * o, axis=-1, keepdims=True)
        y_heads.append((o * lax.rsqrt(ms_h + EPS) * hng[:, sl] * ops.gate[:, sl]).astype(BF16))

    y = jnp.concatenate([ops.yc] + y_heads, axis=-1)
    mixed = _dot(y, wo_ref[...])
    return _layer_norm(ALPHA * x + mixed, g_ref[...], b_ref[...])


def _mixer_kernel(*refs, n_seq, t_len, valid, n_tiles, n_cast):
    x_ref, cs_ref, hs_ref, lbl_ref, win_ref, cw_ref, cng_ref, hng_ref, gm_ref, wo_ref, g_ref, b_ref = refs[:12]
    cast_in, (o_ref, cso_ref, hso_ref, *cast_out) = refs[12:12 + n_cast], refs[12 + n_cast:15 + 2 * n_cast]
    st_ref, carry_ref = refs[15 + 2 * n_cast:]
    j = pl.program_id(1)
    assert valid == t_len or n_tiles == 1
    _cast_chunks(cast_in, cast_out)

    @pl.when(j == 0)
    def _():
        carry_ref[...] = cs_ref[...]
        for g in range(n_seq):
            for hd in range(H_HGRN):
                st_ref[g, hd] = hs_ref[g, hd].T

    x = x_ref[...].reshape(n_seq * t_len, D_MODEL)
    next_prev, ops = _mixer_front(x, carry_ref[...], lbl_ref, win_ref, cw_ref, cng_ref, gm_ref,
                                  n_seq=n_seq, t_len=t_len, valid=valid)
    out = _mixer_back(ops, x, st_ref, hng_ref, wo_ref, g_ref, b_ref, n_seq=n_seq, t_len=t_len)
    o_ref[...] = out.reshape(n_seq, t_len, D_MODEL)
    carry_ref[...] = next_prev

    @pl.when(j == n_tiles - 1)
    def _():
        cso_ref[...] = next_prev
        for g in range(n_seq):
            for hd in range(H_HGRN):
                hso_ref[g, hd] = st_ref[g, hd].T


def _mixer_ln(x, conv_state, hgrn_state, valid, n_seq, t_len, lbl, win, cw, cng, hng, gm, wo, g, b,
              cast=(), layer=0):
    bsz, seq_len, _ = x.shape
    assert bsz % n_seq == 0 and seq_len % t_len == 0
    n_tiles = seq_len // t_len
    n_groups = bsz // n_seq
    kern = functools.partial(_mixer_kernel, n_seq=n_seq, t_len=t_len, valid=valid - (seq_len - t_len),
                             n_tiles=n_tiles, n_cast=len(cast))
    seq_block = lambda *tail: pl.BlockSpec((n_seq,) + tail, lambda i, j: (i,) + (0,) * len(tail))
    tile_block = pl.BlockSpec((n_seq, t_len, D_MODEL), lambda i, j: (i, j, 0))
    cast_in, cast_out, cast_shapes = _cast_specs(cast, layer, n_groups * n_tiles, lambda i, j: i * n_tiles + j)
    y, conv_rows, state, *casted = pl.pallas_call(
        kern,
        grid=(n_groups, n_tiles),
        in_specs=[
            tile_block, seq_block(SUBLANES, D_CONV), seq_block(H_HGRN, DK, DV),
            _const_spec(lbl.shape), _const_spec(win.shape), _const_spec(cw.shape),
            _const_spec(cng.shape), _const_spec(hng.shape), _const_spec(gm.shape),
            _const_spec(wo.shape), _const_spec(g.shape), _const_spec(b.shape),
        ] + cast_in,
        out_specs=[tile_block, seq_block(SUBLANES, D_CONV), seq_block(H_HGRN, DK, DV)] + cast_out,
        out_shape=[
            jax.ShapeDtypeStruct(x.shape, F32),
            jax.ShapeDtypeStruct((bsz, SUBLANES, D_CONV), F32),
            jax.ShapeDtypeStruct((bsz, H_HGRN, DK, DV), F32),
        ] + cast_shapes,
        scratch_shapes=[pltpu.VMEM((n_seq, H_HGRN, DV, DK), F32), pltpu.VMEM((n_seq, SUBLANES, D_CONV), F32)],
        compiler_params=pltpu.CompilerParams(
            dimension_semantics=("arbitrary", "arbitrary"), vmem_limit_bytes=VMEM_LIMIT),
    )(x, conv_state, hgrn_state, lbl, win, cw, cng, hng, gm, wo, g, b, *cast)
    return y, conv_rows, state, casted


def kernel(x_prompt, x_sample, state_conv, state_hgrn, lb_logits, ln1_g, ln1_b, ffn1_w_up, ffn1_w_down,
           ln2_g, ln2_b, w_in, conv_w, conv_norm_g, hgrn_norm_g, w_o, ln3_g, ln3_b, ffn2_w_up, ffn2_w_down):
    bsz, seq_len, _ = x_prompt.shape
    dec_b, dec_l, _ = x_sample.shape
    l = 0
    row = lambda a: a[l].reshape(1, -1)
    wup1, wdn1 = ffn1_w_up[l].astype(BF16), ffn1_w_down[l].astype(BF16)
    win, wo = w_in[l].astype(BF16), w_o[l].astype(BF16)
    lane_group = jnp.arange(D_CONV) // (D_CONV // N_CONV_GROUPS)
    gm = ((lane_group[:, None] == lane_group[None, :]) * (N_CONV_GROUPS / D_CONV)).astype(BF16)

    def pad_conv(cs):
        return jnp.pad(cs, ((0, 0), (SUBLANES - (CONV_W - 1), 0), (0, 0)))

    assert dec_b * dec_l == FFN_ROWS
    xp, xs = _ffn_ln(x_prompt.reshape(bsz * seq_len, D_MODEL), x_sample.reshape(dec_b * dec_l, D_MODEL),
                     wup1, wdn1, row(ln1_g), row(ln1_b))
    mix_w = (lb_logits, win, conv_w[l], row(conv_norm_g), row(hgrn_norm_g), gm, wo, row(ln2_g), row(ln2_b))

    zero_conv = jnp.zeros((bsz, SUBLANES, D_CONV), F32)
    zero_hgrn = jnp.zeros((bsz, H_HGRN, DK, DV), F32)
    xp, conv_p, hgrn_p, (wup2, wdn2) = _mixer_ln(
        xp.reshape(bsz, seq_len, D_MODEL), zero_conv, zero_hgrn, seq_len, PROMPT_SEQS, PROMPT_CHUNK, *mix_w,
        cast=(ffn2_w_up, ffn2_w_down), layer=l)

    xs = jnp.pad(xs.reshape(dec_b, dec_l, D_MODEL), ((0, 0), (0, SUBLANES - dec_l), (0, 0)))
    xs, conv_s, hgrn_s, _ = _mixer_ln(xs, pad_conv(state_conv[l]), state_hgrn[l], dec_l,
                                      SAMPLE_SEQS, SUBLANES, *mix_w)

    yp, ys = _ffn_ln(xp.reshape(bsz * seq_len, D_MODEL), xs[:, :dec_l].reshape(dec_b * dec_l, D_MODEL),
                     wup2, wdn2, row(ln3_g), row(ln3_b))

    return (yp.reshape(bsz, seq_len, D_MODEL), ys.reshape(dec_b, dec_l, D_MODEL),
            conv_p[None, :, 1 - CONV_W:], hgrn_p[None], conv_s[None, :, 1 - CONV_W:], hgrn_s[None])
```

```python
import functools
import math
import types

import jax
import jax.numpy as jnp
from jax import lax
from jax.experimental import pallas as pl
from jax.experimental.pallas import tpu as pltpu

D_MODEL = 1024
D_CONV = 512
N_CONV_GROUPS = 8
CONV_W = 3
H_HGRN = 4
DK = 128
DV = 128
HK = H_HGRN * DK
D_FF = 2816
ALPHA = 2.0 ** 0.25
EPS = 1e-5

SUBLANES = 8
LANES = 128
BF16_SUBLANES = 16
MXU_COLS = 256
VMEM_LIMIT = 56 * 1024 * 1024

SCORE_TILE = LANES
LEVEL_BLOCK = 64
FF_TILE = MXU_COLS
FFN_ROWS = 512
LN_CHUNKS = 8
PROMPT_CHUNK = 256
PROMPT_SEQS = 4
SAMPLE_SEQS = 16

F32 = jnp.float32
BF16 = jnp.bfloat16


def _dot(a, b):
    return jnp.dot(a, b, preferred_element_type=F32)


def _dot_nt(a, b):
    return lax.dot_general(a, b, (((1,), (1,)), ((), ())), preferred_element_type=F32)


def _dot_tn(a, b):
    return lax.dot_general(a, b, (((0,), (0,)), ((), ())), preferred_element_type=F32)


def _sigmoid(x):
    return 0.5 * jnp.tanh(0.5 * x) + 0.5


def _layer_norm(z, g, b):
    mu = jnp.mean(z, axis=-1, keepdims=True)
    d = z - mu
    var = jnp.mean(d * d, axis=-1, keepdims=True)
    return d * lax.rsqrt(var + EPS) * g + b


def _always_zero(y):
    rows, cols = y.shape
    bits = pltpu.bitcast(y, jnp.uint32).reshape(rows // SUBLANES, SUBLANES, cols)
    merged = functools.reduce(jnp.bitwise_or, [bits[i] for i in range(rows // SUBLANES)])
    merged = functools.reduce(jnp.bitwise_or, [merged[:, i:i + LANES] for i in range(0, cols, LANES)])
    return ((merged >> 16) >> 16).astype(F32)


def _const_spec(shape):
    nd = len(shape)
    return pl.BlockSpec(shape, lambda *_: (0,) * nd, pipeline_mode=pl.Buffered(1))


def _cast_specs(arrays, layer, n_steps, step_of):
    in_specs, out_specs, shapes = [], [], []
    for a in arrays:
        _, rows, cols = a.shape
        share = next(k for k in range(1, n_steps + 1)
                     if n_steps % k == 0 and rows % (n_steps // k) == 0 and
                     (rows // (n_steps // k)) % BF16_SUBLANES == 0)
        chunk = rows // (n_steps // share)
        in_specs.append(pl.BlockSpec((None, chunk, cols),
                                     lambda *idx, share=share: (layer, step_of(*idx) // share, 0)))
        out_specs.append(pl.BlockSpec((chunk, cols), lambda *idx, share=share: (step_of(*idx) // share, 0)))
        shapes.append(jax.ShapeDtypeStruct((rows, cols), BF16))
    return in_specs, out_specs, shapes


def _cast_chunks(src_refs, dst_refs):
    for src, dst in zip(src_refs, dst_refs):
        dst[...] = src[...].astype(BF16)


def _ffn_ln_kernel(*refs, n_tiles, n_cast):
    x_ref, xe_ref, wup_ref, wdn_ref, g_ref, b_ref = refs[:6]
    cast_in, (o_ref, oe_ref, *cast_out) = refs[6:6 + n_cast], refs[6 + n_cast:8 + 2 * n_cast]
    z_refs = refs[8 + 2 * n_cast:]
    s = pl.program_id(0)
    tm = x_ref.shape[0]
    chunk = tm // LN_CHUNKS
    _cast_chunks(cast_in, cast_out)

    def finish(z_ref, dst_ref, c):
        rows = slice(c * chunk, (c + 1) * chunk)
        y = _layer_norm(z_ref[rows, :], g_ref[...], b_ref[...])
        dst_ref[rows, :] = y
        return y

    def step(src_ref, read_ref, write_ref):
        x = src_ref[...]
        xb = x.astype(BF16)
        acc = jnp.zeros(x.shape, F32)
        for j in range(D_FF // FF_TILE):
            c = j * FF_TILE
            a = _dot(xb, wup_ref[:, c:c + FF_TILE])
            b = _dot(xb, wup_ref[:, D_FF + c:D_FF + c + FF_TILE])
            h = a * _sigmoid(a) * b
            if j < LN_CHUNKS:
                zero = _always_zero(finish(read_ref, o_ref, j))
                h = jnp.concatenate([h[:SUBLANES] + jnp.concatenate([zero] * (FF_TILE // LANES), axis=1),
                                     h[SUBLANES:]], axis=0)
            acc = acc + _dot(h.astype(BF16), wdn_ref[c:c + FF_TILE, :])
        write_ref[...] = ALPHA * x + 0.5 * acc

    @pl.when(s == 0)
    def _():
        z_refs[1][...] = jnp.zeros(z_refs[1].shape, F32)

    for parity in range(2):
        pl.when((s < n_tiles) & (s % 2 == parity))(
            functools.partial(step, x_ref, z_refs[1 - parity], z_refs[parity]))
    pl.when(s == n_tiles)(
        functools.partial(step, xe_ref, z_refs[(n_tiles - 1) % 2], z_refs[n_tiles % 2]))

    @pl.when(s == n_tiles + 1)
    def _():
        for c in range(LN_CHUNKS):
            finish(z_refs[n_tiles % 2], oe_ref, c)


def _ffn_ln(x, x_extra, wup, wdn, g, b, cast=(), layer=0):
    m = x.shape[0]
    tm = x_extra.shape[0]
    assert m % tm == 0
    n_tiles = m // tm
    tile = lambda s: jnp.clip(s, 0, n_tiles - 1)
    cast_in, cast_out, cast_shapes = _cast_specs(cast, layer, n_tiles, tile)
    y, y_extra, *casted = pl.pallas_call(
        functools.partial(_ffn_ln_kernel, n_tiles=n_tiles, n_cast=len(cast)),
        grid=(n_tiles + 2,),
        in_specs=[
            pl.BlockSpec((tm, D_MODEL), lambda s: (tile(s), 0)),
            pl.BlockSpec((tm, D_MODEL), lambda s: (0, 0)),
            _const_spec(wup.shape),
            _const_spec(wdn.shape),
            _const_spec(g.shape),
            _const_spec(b.shape),
        ] + cast_in,
        out_specs=[pl.BlockSpec((tm, D_MODEL), lambda s: (tile(s - 1), 0)),
                   pl.BlockSpec((tm, D_MODEL), lambda s: (0, 0))] + cast_out,
        out_shape=[jax.ShapeDtypeStruct((m, D_MODEL), F32), jax.ShapeDtypeStruct((tm, D_MODEL), F32)] + cast_shapes,
        scratch_shapes=[pltpu.VMEM((tm, D_MODEL), F32), pltpu.VMEM((tm, D_MODEL), F32)],
        compiler_params=pltpu.CompilerParams(
            dimension_semantics=("arbitrary",), vmem_limit_bytes=VMEM_LIMIT),
    )(x, x_extra, wup, wdn, g, b, *cast)
    return y, y_extra, casted


def _split_halves(a, h):
    rows, lanes = a.shape
    return jnp.split(a.reshape(rows // (2 * h), 2, h, lanes), 2, axis=1)


def _join_halves(lo, hi):
    n, _, h, lanes = lo.shape
    return jnp.concatenate([lo, hi], axis=1).reshape(n * 2 * h, lanes)


def _decay_step(q_pre, k_post, blk, h):
    rows, lanes = blk.shape
    if h < SUBLANES:
        groups = lambda a: a.reshape(rows // SUBLANES, SUBLANES, lanes)
        later = ((lax.broadcasted_iota(jnp.int32, (1, SUBLANES, 1), 1) // h) % 2) == 1
        other = pltpu.roll(groups(blk), h, 1)
        if 2 * h < SUBLANES:
            other = jnp.where(later, other, pltpu.roll(groups(blk), SUBLANES - h, 1))
        q_pre = (groups(q_pre) * jnp.where(later, other, 1.0)).reshape(rows, lanes)
        k_post = (groups(k_post) * jnp.where(later, 1.0, other)).reshape(rows, lanes)
        blk = (groups(blk) * other).reshape(rows, lanes)
    else:
        (q_lo, q_hi), (k_lo, k_hi), (b_lo, b_hi) = (_split_halves(a, h) for a in (q_pre, k_post, blk))
        q_pre = _join_halves(q_lo, q_hi * b_lo)
        k_post = _join_halves(k_lo * b_hi, k_hi)
        blk = _join_halves(b_lo * b_hi, b_lo * b_hi)
    return q_pre, k_post, blk


def _mixer_front(x, prev, lbl_ref, win_ref, cw_ref, cng_ref, gm_ref, *, n_seq, t_len, valid):
    rows = n_seq * t_len
    log2_c = int(math.log2(t_len))
    assert 1 << log2_c == t_len and t_len % SUBLANES == 0 and valid >= CONV_W - 1
    assert rows % LEVEL_BLOCK == 0 and (t_len <= LEVEL_BLOCK or t_len % LEVEL_BLOCK == 0)
    xb = x.astype(BF16)

    def proj(k):
        return _dot(xb, win_ref[:, k * D_CONV:(k + 1) * D_CONV])

    u = (proj(1) * proj(0)).reshape(n_seq, t_len, D_CONV)
    first = lax.broadcasted_iota(jnp.int32, (1, SUBLANES, 1), 1) == 0

    def delay(a, before):
        r = pltpu.roll(a, 1, 1)
        head = jnp.where(first, before, r[:, :SUBLANES])
        return head if t_len == SUBLANES else jnp.concatenate([head, r[:, SUBLANES:]], axis=1)

    cw = cw_ref[...]
    w0, w1, w2 = cw[0:1], cw[1:2], cw[2:3]
    pair = w0 * delay(u, prev[:, SUBLANES - 1:]) + w1 * u
    pair_before = w0 * prev[:, SUBLANES - 2:SUBLANES - 1] + w1 * prev[:, SUBLANES - 1:]
    conv = (delay(pair, pair_before) + w2 * u).reshape(rows, D_CONV)
    pad8 = t_len - valid
    assert pad8 < SUBLANES
    last8 = u[:, t_len - SUBLANES:]
    next_prev = last8 if pad8 == 0 else pltpu.roll(last8, pad8, 1)
    z = proj(2) * conv
    zz = z * z
    zz_hi = zz.astype(BF16)
    zz_lo = (zz - zz_hi.astype(F32)).astype(BF16)
    gm = gm_ref[...]
    ms = _dot(zz_hi, gm) + _dot(zz_lo, gm)
    y_conv = z * lax.rsqrt(ms + EPS) * cng_ref[...]

    lbl = lbl_ref[...]
    lmax = jnp.max(lbl, axis=0, keepdims=True)
    le = jnp.exp(lbl - lmax)
    lb = le[0:1] / jnp.sum(le, axis=0, keepdims=True)
    q = proj(3)
    qf = q * _sigmoid(q)
    f = lb + (1.0 - lb) * _sigmoid(proj(4))
    kk = 1.0 - f
    v = proj(5)
    gate_in = proj(6)
    gate = gate_in * _sigmoid(gate_in)
    if valid < t_len:
        live = lax.broadcasted_iota(jnp.int32, (rows, 1), 0) % t_len < valid
        f = jnp.where(live, f, 1.0)
        kk = jnp.where(live, kk, 0.0)

    return next_prev, types.SimpleNamespace(qf=qf, kk=kk, f=f, v=v, gate=gate, yc=y_conv.astype(BF16))


def _score_tile(qf, kk, f, t_len, level_rows):
    n_group = SCORE_TILE // SUBLANES
    top = min(t_len, SCORE_TILE)
    sc = [None] * n_group

    def place(level, p, takes):
        for n, i in enumerate(takes):
            hit = level_rows[i] == level
            part = p[n * SUBLANES:(n + 1) * SUBLANES]
            sc[i] = jnp.where(hit, part, 0.0 if sc[i] is None else sc[i])

    place(-1, _dot_nt(qf.astype(BF16), kk.astype(BF16)), range(n_group))
    blocks = [(qf[r:r + LEVEL_BLOCK] * f[r:r + LEVEL_BLOCK], kk[r:r + LEVEL_BLOCK], f[r:r + LEVEL_BLOCK])
              for r in range(0, SCORE_TILE, LEVEL_BLOCK)]
    for lh in range(int(math.log2(top))):
        h = 1 << lh
        if 2 * h > blocks[0][0].shape[0]:
            blocks = [tuple(jnp.concatenate([b[i] for b in blocks], axis=0) for i in range(3))]
        k_all = jnp.concatenate([b[1] for b in blocks], axis=0).astype(BF16)
        if h < SUBLANES:
            q_all = jnp.concatenate([b[0] for b in blocks], axis=0).astype(BF16)
            place(lh, _dot_nt(q_all, k_all), range(n_group))
        else:
            q_later = jnp.concatenate([_split_halves(b[0], h)[1].reshape(-1, DK) for b in blocks], axis=0)
            takes = [i for i in range(n_group) if (i * SUBLANES // h) % 2 == 1]
            place(lh, _dot_nt(q_later.astype(BF16), k_all), takes)
        blocks = [_decay_step(*b, h) for b in blocks]
    q_pre, k_post, blk = (jnp.concatenate([b[i] for b in blocks], axis=0) for i in range(3))
    return jnp.concatenate(sc, axis=0).astype(BF16), q_pre, k_post, blk


def _mixer_back(ops, x, st_ref, hng_ref, wo_ref, g_ref, b_ref, *, n_seq, t_len):
    rows = n_seq * t_len
    assert rows % SCORE_TILE == 0 and (t_len <= SCORE_TILE or t_len == 2 * SCORE_TILE)

    ti = lax.broadcasted_iota(jnp.int32, (SCORE_TILE, SCORE_TILE), 0)
    si = lax.broadcasted_iota(jnp.int32, (SCORE_TILE, SCORE_TILE), 1)
    txs = ti ^ si
    pair_level = jnp.where(ti > si, 0, jnp.where(ti == si, -1, -2))
    for lh in range(1, int(math.log2(SCORE_TILE))):
        pair_level = pair_level + jnp.where((ti > si) & (txs >= (1 << lh)), 1, 0)
    level_rows = [pair_level[i * SUBLANES:(i + 1) * SUBLANES] for i in range(SCORE_TILE // SUBLANES)]

    hng = hng_ref[...]
    y_heads = []
    for hd in range(H_HGRN):
        sl = slice(hd * DK, (hd + 1) * DK)
        vb = ops.v[:, sl].astype(BF16)
        tiles = [_score_tile(ops.qf[r:r + SCORE_TILE, sl], ops.kk[r:r + SCORE_TILE, sl],
                             ops.f[r:r + SCORE_TILE, sl], t_len, level_rows)
                 for r in range(0, rows, SCORE_TILE)]
        o_parts, qc, kc, decay = [], [], [], []
        if t_len <= SCORE_TILE:
            for it, (sc, q_pre, k_post, blk) in enumerate(tiles):
                o_parts.append(_dot(sc, vb[it * SCORE_TILE:(it + 1) * SCORE_TILE]))
                qc.append(q_pre)
                kc.append(k_post)
                decay += [blk[r:r + 1] for r in range(0, SCORE_TILE, t_len)]
        else:
            for g in range(n_seq):
                (sc0, q0, k0, b0), (sc1, q1, k1, b1) = tiles[2 * g], tiles[2 * g + 1]
                cross = _dot_nt(q1.astype(BF16), k0.astype(BF16)).astype(BF16)
                o_parts.append(_dot(sc0, vb[g * t_len:g * t_len + SCORE_TILE]))
                o_parts.append(_dot(jnp.concatenate([cross, sc1], axis=1), vb[g * t_len:(g + 1) * t_len]))
                qc += [q0, q1 * b0[0:1]]
                kc += [k0 * b1[0:1], k1]
                decay.append(b0[0:1] * b1[0:1])
        cat = lambda parts: parts[0] if len(parts) == 1 else jnp.concatenate(parts, axis=0)
        o, qc, kc = cat(o_parts), cat(qc).astype(BF16), cat(kc).astype(BF16)
        inter = []
        for g in range(n_seq):
            rs = slice(g * t_len, (g + 1) * t_len)
            st = st_ref[g, hd]
            inter.append(_dot_nt(qc[rs], st.astype(BF16)))
            st_ref[g, hd] = st * decay[g] + _dot_tn(vb[rs], kc[rs])
        o = o + cat(inter)
        ms_h = jnp.mean(o * o, axis=-1, keepdims=True)
        y_heads.append((o * lax.rsqrt(ms_h + EPS) * hng[:, sl] * ops.gate[:, sl]).astype(BF16))

    y = jnp.concatenate([ops.yc] + y_heads, axis=-1)
    mixed = _dot(y, wo_ref[...])
    return _layer_norm(ALPHA * x + mixed, g_ref[...], b_ref[...])


def _mixer_kernel(*refs, n_seq, t_len, valid, n_tiles, n_cast):
    x_ref, cs_ref, hs_ref, lbl_ref, win_ref, cw_ref, cng_ref, hng_ref, gm_ref, wo_ref, g_ref, b_ref = refs[:12]
    cast_in, (o_ref, cso_ref, hso_ref, *cast_out) = refs[12:12 + n_cast], refs[12 + n_cast:15 + 2 * n_cast]
    st_ref, carry_ref = refs[15 + 2 * n_cast:]
    j = pl.program_id(1)
    assert valid == t_len or n_tiles == 1
    _cast_chunks(cast_in, cast_out)

    @pl.when(j == 0)
    def _():
        carry_ref[...] = cs_ref[...]
        for g in range(n_seq):
            for hd in range(H_HGRN):
                st_ref[g, hd] = hs_ref[g, hd].T

    x = x_ref[...].reshape(n_seq * t_len, D_MODEL)
    next_prev, ops = _mixer_front(x, carry_ref[...], lbl_ref, win_ref, cw_ref, cng_ref, gm_ref,
                                  n_seq=n_seq, t_len=t_len, valid=valid)
    out = _mixer_back(ops, x, st_ref, hng_ref, wo_ref, g_ref, b_ref, n_seq=n_seq, t_len=t_len)
    o_ref[...] = out.reshape(n_seq, t_len, D_MODEL)
    carry_ref[...] = next_prev

    @pl.when(j == n_tiles - 1)
    def _():
        cso_ref[...] = next_prev
        for g in range(n_seq):
            for hd in range(H_HGRN):
                hso_ref[g, hd] = st_ref[g, hd].T


def _mixer_ln(x, conv_state, hgrn_state, valid, n_seq, t_len, lbl, win, cw, cng, hng, gm, wo, g, b,
              cast=(), layer=0):
    bsz, seq_len, _ = x.shape
    assert bsz % n_seq == 0 and seq_len % t_len == 0
    n_tiles = seq_len // t_len
    n_groups = bsz // n_seq
    kern = functools.partial(_mixer_kernel, n_seq=n_seq, t_len=t_len, valid=valid - (seq_len - t_len),
                             n_tiles=n_tiles, n_cast=len(cast))
    seq_block = lambda *tail: pl.BlockSpec((n_seq,) + tail, lambda i, j: (i,) + (0,) * len(tail))
    tile_block = pl.BlockSpec((n_seq, t_len, D_MODEL), lambda i, j: (i, j, 0))
    cast_in, cast_out, cast_shapes = _cast_specs(cast, layer, n_groups * n_tiles, lambda i, j: i * n_tiles + j)
    y, conv_rows, state, *casted = pl.pallas_call(
        kern,
        grid=(n_groups, n_tiles),
        in_specs=[
            tile_block, seq_block(SUBLANES, D_CONV), seq_block(H_HGRN, DK, DV),
            _const_spec(lbl.shape), _const_spec(win.shape), _const_spec(cw.shape),
            _const_spec(cng.shape), _const_spec(hng.shape), _const_spec(gm.shape),
            _const_spec(wo.shape), _const_spec(g.shape), _const_spec(b.shape),
        ] + cast_in,
        out_specs=[tile_block, seq_block(SUBLANES, D_CONV), seq_block(H_HGRN, DK, DV)] + cast_out,
        out_shape=[
            jax.ShapeDtypeStruct(x.shape, F32),
            jax.ShapeDtypeStruct((bsz, SUBLANES, D_CONV), F32),
            jax.ShapeDtypeStruct((bsz, H_HGRN, DK, DV), F32),
        ] + cast_shapes,
        scratch_shapes=[pltpu.VMEM((n_seq, H_HGRN, DV, DK), F32), pltpu.VMEM((n_seq, SUBLANES, D_CONV), F32)],
        compiler_params=pltpu.CompilerParams(
            dimension_semantics=("arbitrary", "arbitrary"), vmem_limit_bytes=VMEM_LIMIT),
    )(x, conv_state, hgrn_state, lbl, win, cw, cng, hng, gm, wo, g, b, *cast)
    return y, conv_rows, state, casted


def kernel(x_prompt, x_sample, state_conv, state_hgrn, lb_logits, ln1_g, ln1_b, ffn1_w_up, ffn1_w_down,
           ln2_g, ln2_b, w_in, conv_w, conv_norm_g, hgrn_norm_g, w_o, ln3_g, ln3_b, ffn2_w_up, ffn2_w_down):
    bsz, seq_len, _ = x_prompt.shape
    dec_b, dec_l, _ = x_sample.shape
    l = 0
    row = lambda a: a[l].reshape(1, -1)
    wup1, wdn1 = ffn1_w_up[l].astype(BF16), ffn1_w_down[l].astype(BF16)
    lane_group = jnp.arange(D_CONV) // (D_CONV // N_CONV_GROUPS)
    gm = ((lane_group[:, None] == lane_group[None, :]) * (N_CONV_GROUPS / D_CONV)).astype(BF16)

    def pad_conv(cs):
        return jnp.pad(cs, ((0, 0), (SUBLANES - (CONV_W - 1), 0), (0, 0)))

    assert dec_b * dec_l == FFN_ROWS
    xp, xs, (win, wo) = _ffn_ln(x_prompt.reshape(bsz * seq_len, D_MODEL),
                                x_sample.reshape(dec_b * dec_l, D_MODEL),
                                wup1, wdn1, row(ln1_g), row(ln1_b), cast=(w_in, w_o), layer=l)
    mix_w = (lb_logits, win, conv_w[l], row(conv_norm_g), row(hgrn_norm_g), gm, wo, row(ln2_g), row(ln2_b))

    zero_conv = jnp.zeros((bsz, SUBLANES, D_CONV), F32)
    zero_hgrn = jnp.zeros((bsz, H_HGRN, DK, DV), F32)
    xp, conv_p, hgrn_p, (wup2, wdn2) = _mixer_ln(
        xp.reshape(bsz, seq_len, D_MODEL), zero_conv, zero_hgrn, seq_len, PROMPT_SEQS, PROMPT_CHUNK, *mix_w,
        cast=(ffn2_w_up, ffn2_w_down), layer=l)

    xs = jnp.pad(xs.reshape(dec_b, dec_l, D_MODEL), ((0, 0), (0, SUBLANES - dec_l), (0, 0)))
    xs, conv_s, hgrn_s, _ = _mixer_ln(xs, pad_conv(state_conv[l]), state_hgrn[l], dec_l,
                                      SAMPLE_SEQS, SUBLANES, *mix_w)

    yp, ys, _ = _ffn_ln(xp.reshape(bsz * seq_len, D_MODEL), xs[:, :dec_l].reshape(dec_b * dec_l, D_MODEL),
                        wup2, wdn2, row(ln3_g), row(ln3_b))

    return (yp.reshape(bsz, seq_len, D_MODEL), ys.reshape(dec_b, dec_l, D_MODEL),
            conv_p[None, :, 1 - CONV_W:], hgrn_p[None], conv_s[None, :, 1 - CONV_W:], hgrn_s[None])
```
